```python
import jax, jax.numpy as jnp
from jax import lax
import numpy as np

D_MODEL = 2048
BATCH = 2
SEQ = 8192
DEPTH = 1

ATTN_HEAD_DIM = 64
ATTN_HEADS = D_MODEL // (2 * ATTN_HEAD_DIM)
ATTN_KV_HEADS = max(ATTN_HEADS // 8, 1)
WINDOW = 128
ATTN_BLOCK = 128
ROPE_THETA = 500000.0
ROPE_DIM = ATTN_HEAD_DIM // 4
MLSTM_HEADS = 4
MLSTM_V_DIM = D_MODEL // (2 * MLSTM_HEADS)
MLSTM_QK_DIM = MLSTM_V_DIM // 2
MLSTM_CHUNK = 128
CONV_WIDTH = 4
GATE_SOFTCAP = 15.0
N_EXPERTS = 32
TOP_K = 4
D_FF = D_MODEL
SWIGLU_ALPHA = 1.702
SWIGLU_LIMIT = 7.0
MOE_BLOCK = 128
NORM_EPS = 1e-6

ATTN_Q_W = ATTN_HEADS * ATTN_HEAD_DIM
ATTN_KV_W = ATTN_KV_HEADS * ATTN_HEAD_DIM
MLSTM_QK_W = MLSTM_HEADS * MLSTM_QK_DIM
MLSTM_V_W = MLSTM_HEADS * MLSTM_V_DIM
IN_SIZES = (ATTN_Q_W, ATTN_KV_W, ATTN_KV_W, MLSTM_QK_W, MLSTM_QK_W, MLSTM_V_W, MLSTM_V_W, MLSTM_HEADS, MLSTM_HEADS, D_MODEL, D_MODEL)
IN_WIDTH = sum(IN_SIZES)
FGATE_OFFSET = sum(IN_SIZES[:8])

kernel_name = 'hybrid_swa_mlstm_moe_block'


def rms_norm(x, g):
    xf = x.astype(jnp.float32)
    y = xf * lax.rsqrt(jnp.mean(xf * xf, axis=-1, keepdims=True) + NORM_EPS)
    return (y * g.astype(jnp.float32)).astype(x.dtype)


def partial_rope(x, positions):
    half = ROPE_DIM // 2
    inv_freq = ROPE_THETA ** (-jnp.arange(half, dtype=jnp.float32) / half)
    ang = positions.astype(jnp.float32)[..., None] * inv_freq
    cos = jnp.cos(ang)[:, :, None, :]
    sin = jnp.sin(ang)[:, :, None, :]
    xr = x[..., :ROPE_DIM].astype(jnp.float32)
    x1, x2 = xr[..., :half], xr[..., half:]
    rot = jnp.concatenate([x1 * cos - x2 * sin, x2 * cos + x1 * sin], axis=-1)
    return jnp.concatenate([rot.astype(x.dtype), x[..., ROPE_DIM:]], axis=-1)


def sliding_window_attention(q, k, v, sinks):
    B, S, Hq, Dh = q.shape
    Hkv = k.shape[2]
    G = Hq // Hkv
    L = ATTN_BLOCK
    nb = S // L
    qb = q.reshape(B, nb, L, Hkv, G, Dh)
    kb = k.reshape(B, nb, L, Hkv, Dh)
    vb = v.reshape(B, nb, L, Hkv, Dh)
    zpad = jnp.zeros_like(kb[:, :1])
    k2 = jnp.concatenate([jnp.concatenate([zpad, kb[:, :-1]], axis=1), kb], axis=2)
    v2 = jnp.concatenate([jnp.concatenate([zpad, vb[:, :-1]], axis=1), vb], axis=2)
    s = jnp.einsum('bnqhgd,bnkhd->bnhgqk', qb, k2, preferred_element_type=jnp.float32) * (Dh ** -0.5)
    qi = jnp.arange(L)[:, None]
    kj = jnp.arange(2 * L)[None, :]
    diff = qi + L - kj
    band = (diff >= 0) & (diff < WINDOW)
    blk = jnp.arange(nb)[:, None, None]
    mask = band[None] & ((blk > 0) | (kj >= L)[None])
    s = jnp.where(mask[None, :, None, None], s, -jnp.inf)
    sink = sinks.astype(jnp.float32).reshape(Hkv, G)[None, None, :, :, None, None]
    m = jnp.maximum(jnp.max(s, axis=-1, keepdims=True), sink)
    p = jnp.exp(s - m)
    p = p / (jnp.sum(p, axis=-1, keepdims=True) + jnp.exp(sink - m))
    o = jnp.einsum('bnhgqk,bnkhd->bnqhgd', p.astype(v.dtype), v2)
    return o.reshape(B, S, Hq * Dh)


def causal_depthwise_conv(x, w):
    C = x.shape[-1]
    return lax.conv_general_dilated(x, w.astype(x.dtype)[:, None, :], window_strides=(1,),
                                    padding=[(CONV_WIDTH - 1, 0)],
                                    dimension_numbers=('NWC', 'WIO', 'NWC'),
                                    feature_group_count=C)


def softcap(z):
    return GATE_SOFTCAP * jnp.tanh(z / GATE_SOFTCAP)


def mlstm_chunkwise(q, k, v, i_pre, f_pre):
    B, S, H, dqk = q.shape
    dv = v.shape[-1]
    L = MLSTM_CHUNK
    nc = S // L
    f32 = jnp.float32

    def to_chunks(t):
        return t.astype(f32).reshape(B, nc, L, H, t.shape[-1]).transpose(0, 3, 1, 2, 4)

    qc = to_chunks(q)
    kc = to_chunks(k) * (dqk ** -0.5)
    vc = to_chunks(v)
    log_i = softcap(i_pre.astype(f32)).reshape(B, nc, L, H).transpose(0, 3, 1, 2)
    log_f = jax.nn.log_sigmoid(softcap(f_pre.astype(f32))).reshape(B, nc, L, H).transpose(0, 3, 1, 2)
    b = jnp.cumsum(log_f, axis=-1)
    b_last = b[..., -1]
    a = b_last[..., None] - b + log_i

    def step(carry, xs):
        C, n, m = carry
        k_c, v_c, a_c, g_c = xs
        m_new = jnp.maximum(g_c + m, jnp.max(a_c, axis=-1))
        decay = jnp.exp(g_c + m - m_new)
        wk = jnp.exp(a_c - m_new[..., None])[..., None] * k_c
        C_new = decay[..., None, None] * C + jnp.einsum('bhsk,bhsv->bhkv', wk, v_c)
        n_new = decay[..., None] * n + jnp.sum(wk, axis=2)
        return (C_new, n_new, m_new), (C, n, m)

    init = (jnp.zeros((B, H, dqk, dv), f32), jnp.zeros((B, H, dqk), f32), jnp.zeros((B, H), f32))
    xs = (jnp.moveaxis(kc, 2, 0), jnp.moveaxis(vc, 2, 0), jnp.moveaxis(a, 2, 0), jnp.moveaxis(b_last, 2, 0))
    _, (C_prev, n_prev, m_prev) = lax.scan(step, init, xs)
    C_prev = jnp.moveaxis(C_prev, 0, 2)
    n_prev = jnp.moveaxis(n_prev, 0, 2)
    m_prev = jnp.moveaxis(m_prev, 0, 2)

    causal = jnp.tril(jnp.ones((L, L), dtype=bool))
    d_log = jnp.where(causal, b[..., :, None] - b[..., None, :] + log_i[..., None, :], -jnp.inf)
    inter = b + m_prev[..., None]
    m = jnp.maximum(inter, jnp.max(d_log, axis=-1))
    s_qk = jnp.einsum('bhctk,bhcsk->bhcts', qc, kc) * jnp.exp(d_log - m[..., None])
    inter_w = jnp.exp(inter - m)
    num = jnp.einsum('bhcts,bhcsv->bhctv', s_qk, vc) + inter_w[..., None] * jnp.einsum('bhctk,bhckv->bhctv', qc, C_prev)
    den = jnp.sum(s_qk, axis=-1) + inter_w * jnp.einsum('bhctk,bhck->bhct', qc, n_prev)
    h = num / jnp.maximum(jnp.abs(den), jnp.exp(-m))[..., None]
    return h.transpose(0, 2, 3, 1, 4).reshape(B, S, H, dv)


def hybrid_mixer(h, positions, w_in, b_in, conv_qk, sinks, g_mlstm, w_attn_proj, w_mlstm_proj, w_out):
    B, S, _ = h.shape
    proj = h @ w_in + b_in
    split_idx = np.cumsum(IN_SIZES)[:-1].tolist()
    aq, ak, av, mq, mk, mv, mo, mi, mf, ga, gm = jnp.split(proj, split_idx, axis=-1)
    aq = partial_rope(aq.reshape(B, S, ATTN_HEADS, ATTN_HEAD_DIM), positions)
    ak = partial_rope(ak.reshape(B, S, ATTN_KV_HEADS, ATTN_HEAD_DIM), positions)
    av = av.reshape(B, S, ATTN_KV_HEADS, ATTN_HEAD_DIM)
    y_attn = sliding_window_attention(aq, ak, av, sinks) @ w_attn_proj
    qk = jax.nn.silu(causal_depthwise_conv(jnp.concatenate([mq, mk], axis=-1), conv_qk))
    mq, mk = jnp.split(qk, [MLSTM_QK_W], axis=-1)
    hm = mlstm_chunkwise(mq.reshape(B, S, MLSTM_HEADS, MLSTM_QK_DIM), mk.reshape(B, S, MLSTM_HEADS, MLSTM_QK_DIM),
                         mv.reshape(B, S, MLSTM_HEADS, MLSTM_V_DIM), mi, mf).astype(h.dtype)
    hm = rms_norm(hm, g_mlstm.reshape(MLSTM_HEADS, MLSTM_V_DIM)).reshape(B, S, MLSTM_V_W)
    y_mlstm = (jax.nn.sigmoid(mo) * hm) @ w_mlstm_proj
    merged = jax.nn.sigmoid(ga) * y_attn + jax.nn.sigmoid(gm) * y_mlstm
    return merged @ w_out


def moe_ffn(h, w_router, b_router, w_gate_up, b_gate_up, w_down, b_down):
    B, S, D = h.shape
    N = B * S
    NK = N * TOP_K
    hf = h.reshape(N, D)
    logits = (hf @ w_router).astype(jnp.float32) + b_router.astype(jnp.float32)
    top_logits, top_idx = lax.top_k(logits, TOP_K)
    top_w = jax.nn.softmax(top_logits, axis=-1)
    flat_e = top_idx.reshape(NK)
    flat_tok = jnp.arange(NK, dtype=jnp.int32) // TOP_K
    flat_w = top_w.reshape(NK)
    order = jnp.argsort(flat_e)
    sorted_e = flat_e[order]
    sorted_tok = flat_tok[order]
    sorted_w = flat_w[order]
    counts = jnp.bincount(flat_e, length=N_EXPERTS)
    starts = jnp.cumsum(counts) - counts
    padded = (counts + MOE_BLOCK - 1) // MOE_BLOCK * MOE_BLOCK
    pad_ends = jnp.cumsum(padded)
    pad_starts = pad_ends - padded
    dest = pad_starts[sorted_e] + (jnp.arange(NK) - starts[sorted_e])
    P = NK + N_EXPERTS * MOE_BLOCK
    nblk = P // MOE_BLOCK
    row_tok = jnp.zeros((P,), jnp.int32).at[dest].set(sorted_tok)
    row_w = jnp.zeros((P,), jnp.float32).at[dest].set(sorted_w)
    blk_e = jnp.minimum(jnp.searchsorted(pad_ends, jnp.arange(nblk) * MOE_BLOCK, side='right'), N_EXPERTS - 1)

    def expert_block(args):
        tok, wgt, e = args
        xb = hf[tok]
        gu = xb @ w_gate_up[e] + b_gate_up[e]
        glu = jnp.minimum(gu[:, 0::2], SWIGLU_LIMIT)
        lin = jnp.clip(gu[:, 1::2], -SWIGLU_LIMIT, SWIGLU_LIMIT)
        act = glu * jax.nn.sigmoid(SWIGLU_ALPHA * glu) * (lin + 1)
        y = act @ w_down[e] + b_down[e]
        return y.astype(jnp.float32) * wgt[:, None]

    y_rows = lax.map(expert_block, (row_tok.reshape(nblk, MOE_BLOCK), row_w.reshape(nblk, MOE_BLOCK), blk_e))
    out = jnp.zeros((N, D), jnp.float32).at[row_tok].add(y_rows.reshape(P, D))
    return out.astype(h.dtype).reshape(B, S, D)


def setup_inputs(seed: int = 0) -> dict:
    key = jax.random.key(seed)
    ks = jax.random.split(key, 24)
    D = D_MODEL
    nrm = jax.random.normal
    x = nrm(ks[0], (BATCH, SEQ, D), jnp.float32)
    c = nrm(ks[1], (BATCH, D), jnp.float32)
    offset = jax.random.randint(ks[2], (BATCH, 1), 0, 1024, dtype=jnp.int32)
    positions = offset + jnp.arange(SEQ, dtype=jnp.int32)[None, :]
    w_ada = nrm(ks[3], (DEPTH, D, 6 * D), jnp.float32) * D ** -0.5
    b_ada = 0.01 * nrm(ks[4], (DEPTH, 6 * D), jnp.float32)
    g_pre_mix = 1.0 + 0.02 * nrm(ks[5], (DEPTH, D), jnp.float32)
    g_post_mix = 1.0 + 0.02 * nrm(ks[6], (DEPTH, D), jnp.float32)
    g_pre_ffn = 1.0 + 0.02 * nrm(ks[7], (DEPTH, D), jnp.float32)
    g_post_ffn = 1.0 + 0.02 * nrm(ks[8], (DEPTH, D), jnp.float32)
    w_in = nrm(ks[9], (DEPTH, D, IN_WIDTH), jnp.float32) * D ** -0.5
    fbias = jnp.linspace(3.0, 6.0, MLSTM_HEADS, dtype=jnp.float32)
    b_in = (0.01 * nrm(ks[10], (DEPTH, IN_WIDTH), jnp.float32)).at[:, FGATE_OFFSET:FGATE_OFFSET + MLSTM_HEADS].add(fbias)
    conv_qk = nrm(ks[11], (DEPTH, CONV_WIDTH, 2 * MLSTM_QK_W), jnp.float32) * CONV_WIDTH ** -0.5
    sinks = nrm(ks[12], (DEPTH, ATTN_HEADS), jnp.float32)
    g_mlstm = 1.0 + 0.02 * nrm(ks[13], (DEPTH, MLSTM_V_W), jnp.float32)
    w_attn_proj = nrm(ks[14], (DEPTH, ATTN_Q_W, D), jnp.float32) * ATTN_Q_W ** -0.5
    w_mlstm_proj = nrm(ks[15], (DEPTH, MLSTM_V_W, D), jnp.float32) * MLSTM_V_W ** -0.5
    w_out = nrm(ks[16], (DEPTH, D, D), jnp.float32) * D ** -0.5
    w_router = nrm(ks[17], (DEPTH, D, N_EXPERTS), jnp.float32) * D ** -0.5
    b_router = 0.01 * nrm(ks[18], (DEPTH, N_EXPERTS), jnp.float32)
    w_gate_up = nrm(ks[19], (DEPTH, N_EXPERTS, D, 2 * D_FF), jnp.float32) * D ** -0.5
    b_gate_up = 0.01 * nrm(ks[20], (DEPTH, N_EXPERTS, 2 * D_FF), jnp.float32)
    w_down = nrm(ks[21], (DEPTH, N_EXPERTS, D_FF, D), jnp.float32) * D_FF ** -0.5
    b_down = 0.01 * nrm(ks[22], (DEPTH, N_EXPERTS, D), jnp.float32)
    return {'x': x, 'c': c, 'positions': positions, 'w_ada': w_ada, 'b_ada': b_ada,
            'g_pre_mix': g_pre_mix, 'g_post_mix': g_post_mix, 'g_pre_ffn': g_pre_ffn, 'g_post_ffn': g_post_ffn,
            'w_in': w_in, 'b_in': b_in, 'conv_qk': conv_qk, 'sinks': sinks, 'g_mlstm': g_mlstm,
            'w_attn_proj': w_attn_proj, 'w_mlstm_proj': w_mlstm_proj, 'w_out': w_out,
            'w_router': w_router, 'b_router': b_router, 'w_gate_up': w_gate_up, 'b_gate_up': b_gate_up,
            'w_down': w_down, 'b_down': b_down}


def reference(x, c, positions, w_ada, b_ada, g_pre_mix, g_post_mix, g_pre_ffn, g_post_ffn,
              w_in, b_in, conv_qk, sinks, g_mlstm, w_attn_proj, w_mlstm_proj, w_out,
              w_router, b_router, w_gate_up, b_gate_up, w_down, b_down):
    cond = jax.nn.silu(c)
    for l in range(DEPTH):
        mod = cond @ w_ada[l] + b_ada[l]
        sh1, sc1, gt1, sh2, sc2, gt2 = [t[:, None, :] for t in jnp.split(mod, 6, axis=-1)]
        h = rms_norm(x, g_pre_mix[l]) * (1 + sc1) + sh1
        y = hybrid_mixer(h, positions, w_in[l], b_in[l], conv_qk[l], sinks[l], g_mlstm[l],
                         w_attn_proj[l], w_mlstm_proj[l], w_out[l])
        x = x + gt1 * rms_norm(y, g_post_mix[l])
        h = rms_norm(x, g_pre_ffn[l]) * (1 + sc2) + sh2
        y = moe_ffn(h, w_router[l], b_router[l], w_gate_up[l], b_gate_up[l], w_down[l], b_down[l])
        x = x + gt2 * rms_norm(y, g_post_ffn[l])
    return x
```

```python
import functools

import jax
import jax.numpy as jnp
from jax import lax
from jax.experimental import pallas as pl
from jax.experimental.pallas import tpu as pltpu

F32 = jnp.float32
BF16 = jnp.bfloat16
I32 = jnp.int32
HIGHEST = lax.Precision.HIGHEST

ATTN_HEAD_DIM = 64
ATTN_GROUP = 8
WINDOW = 128
ATTN_BLOCK = 128
ROPE_THETA = 500000.0
ROPE_DIM = ATTN_HEAD_DIM // 4
MLSTM_HEADS = 4
MLSTM_CHUNK = 128
CONV_WIDTH = 4
GATE_SOFTCAP = 15.0
N_EXPERTS = 32
TOP_K = 4
SWIGLU_ALPHA = 1.702
SWIGLU_LIMIT = 7.0
NORM_EPS = 1e-6

LANES = 128
SUBLANES = 8
VMEM_LIMIT = 56 * 1024 * 1024

NEG_INF = float("-inf")


def _cparams(*sem):
    return pltpu.CompilerParams(dimension_semantics=sem, vmem_limit_bytes=VMEM_LIMIT)


def _rms(x, g):
    ms = jnp.mean(x * x, axis=-1, keepdims=True)
    return x * lax.rsqrt(ms + NORM_EPS) * g


def _sigmoid(x):
    return 1.0 / (1.0 + jnp.exp(-x))


def _ada_body(c_ref, w_ref, b_ref, o_ref):
    c = c_ref[...]
    cond = c * _sigmoid(c)
    o_ref[...] = jnp.dot(cond, w_ref[...], preferred_element_type=F32, precision=HIGHEST) + b_ref[...]


def _ada_ln(c, w_ada, b_ada):
    B, D = c.shape
    W = w_ada.shape[1]
    tn = 1024
    cp = jnp.zeros((SUBLANES, D), F32).at[:B].set(c)
    out = pl.pallas_call(
        _ada_body,
        grid=(W // tn,),
        in_specs=[pl.BlockSpec((SUBLANES, D), lambda j: (0, 0)),
                  pl.BlockSpec((D, tn), lambda j: (0, j)),
                  pl.BlockSpec((1, tn), lambda j: (0, j))],
        out_specs=pl.BlockSpec((SUBLANES, tn), lambda j: (0, j)),
        out_shape=jax.ShapeDtypeStruct((SUBLANES, W), F32),
        compiler_params=_cparams("arbitrary"),
        name="ada_ln",
    )(cp, w_ada, b_ada.reshape(1, W))
    return out[:B]


def _inproj_body(x_ref, mod_ref, g_ref, w_ref, b_ref, wg_ref, bg_ref, o_ref, og_ref, h_ref):
    @pl.when(pl.program_id(1) == 0)
    def _():
        h = _rms(x_ref[...], g_ref[...]) * (1.0 + mod_ref[0, 1:2, :]) + mod_ref[0, 0:1, :]
        hb = h.astype(BF16)
        h_ref[...] = hb
        og_ref[...] = jnp.dot(hb, wg_ref[...], preferred_element_type=F32) + bg_ref[...]

    acc = jnp.dot(h_ref[...], w_ref[...], preferred_element_type=F32) + b_ref[...]
    o_ref[...] = acc.astype(BF16)


def _in_proj(x2, mod3, g, w_main, b_main, w_gates, b_gates, rows_per_batch, tm=1024, tn=768):
    N, D = x2.shape
    W = w_main.shape[1]
    tpb = rows_per_batch // tm
    return pl.pallas_call(
        _inproj_body,
        grid=(N // tm, W // tn),
        in_specs=[pl.BlockSpec((tm, D), lambda i, j: (i, 0)),
                  pl.BlockSpec((1, 6, D), lambda i, j: (i // tpb, 0, 0)),
                  pl.BlockSpec((1, D), lambda i, j: (0, 0)),
                  pl.BlockSpec((D, tn), lambda i, j: (0, j)),
                  pl.BlockSpec((1, tn), lambda i, j: (0, j)),
                  pl.BlockSpec((D, LANES), lambda i, j: (0, 0)),
                  pl.BlockSpec((1, LANES), lambda i, j: (0, 0))],
        out_specs=[pl.BlockSpec((tm, tn), lambda i, j: (i, j)),
                   pl.BlockSpec((tm, LANES), lambda i, j: (i, 0))],
        out_shape=[jax.ShapeDtypeStruct((N, W), BF16),
                   jax.ShapeDtypeStruct((N, LANES), F32)],
        scratch_shapes=[pltpu.VMEM((tm, D), BF16)],
        compiler_params=_cparams("arbitrary", "arbitrary"),
        name="in_proj",
    )(x2, mod3, g, w_main, b_main, w_gates, b_gates)


def _attn_body(sinks_ref, q_ref, kc_ref, kp_ref, vc_ref, vp_ref, posc_ref, posp_ref, inv_ref, o_ref):
    n = pl.program_id(1)
    L = ATTN_BLOCK
    inv = inv_ref[...]
    lane = lax.broadcasted_iota(I32, (1, LANES), 1)
    first_half = (lane % ATTN_HEAD_DIM) < (ROPE_DIM // 2)
    low_lanes = lane < ATTN_HEAD_DIM

    ang_c = posc_ref[...] * inv
    ang_p = posp_ref[...] * inv
    cos_c, sin_c = jnp.cos(ang_c), jnp.sin(ang_c)
    cos_p, sin_p = jnp.cos(ang_p), jnp.sin(ang_p)

    def rope(x, cs, sn):
        swapped = jnp.where(first_half, -pltpu.roll(x, LANES - ROPE_DIM // 2, 1), pltpu.roll(x, ROPE_DIM // 2, 1))
        return x * cs + swapped * sn

    k2 = jnp.concatenate([rope(kp_ref[...].astype(F32), cos_p, sin_p),
                          rope(kc_ref[...].astype(F32), cos_c, sin_c)], axis=0)
    v2 = jnp.concatenate([vp_ref[...], vc_ref[...]], axis=0).astype(F32)
    k2s = pltpu.roll(k2, ATTN_HEAD_DIM, 1)
    v2s = pltpu.roll(v2, ATTN_HEAD_DIM, 1)
    zero = jnp.zeros_like(k2)
    k_lo = [jnp.where(low_lanes, k2, zero).astype(BF16), jnp.where(low_lanes, k2s, zero).astype(BF16)]
    k_hi = [jnp.where(low_lanes, zero, k2s).astype(BF16), jnp.where(low_lanes, zero, k2).astype(BF16)]
    v_lo = [jnp.where(low_lanes, v2, zero).astype(BF16), jnp.where(low_lanes, v2s, zero).astype(BF16)]
    v_hi = [jnp.where(low_lanes, zero, v2s).astype(BF16), jnp.where(low_lanes, zero, v2).astype(BF16)]

    qi = lax.broadcasted_iota(I32, (L, 2 * L), 0)
    kj = lax.broadcasted_iota(I32, (L, 2 * L), 1)
    diff = qi + L - kj
    valid = (diff >= 0) & (diff < WINDOW) & ((kj >= L) | (n > 0))

    scale = ATTN_HEAD_DIM ** -0.5
    n_pairs = q_ref.shape[1] // LANES
    for j in range(n_pairs):
        hk = (2 * j) // ATTN_GROUP
        qp = (rope(q_ref[:, j * LANES:(j + 1) * LANES].astype(F32), cos_c, sin_c) * scale).astype(BF16)
        acc = jnp.zeros((L, LANES), F32)
        for half, (kx, vx) in enumerate(((k_lo[hk], v_lo[hk]), (k_hi[hk], v_hi[hk]))):
            s = lax.dot_general(qp, kx, (((1,), (1,)), ((), ())), preferred_element_type=F32)
            s = jnp.where(valid, s, NEG_INF)
            sink = sinks_ref[2 * j + half]
            m = jnp.maximum(jnp.max(s, axis=-1, keepdims=True), sink)
            p = jnp.exp(s - m)
            denom = jnp.sum(p, axis=-1, keepdims=True) + jnp.exp(sink - m)
            acc = acc + jnp.dot(p.astype(BF16), vx, preferred_element_type=F32) / denom
        o_ref[:, j * LANES:(j + 1) * LANES] = acc.astype(BF16)


def _swa_attention(proj, pos_f, inv_lane, sinks, B, S, q_width, k_blk, v_blk):
    L = ATTN_BLOCK
    nb = S // L

    def cur(b, n, s):
        return b * nb + n

    def prev(b, n, s):
        return b * nb + jnp.maximum(n - 1, 0)

    grid_spec = pltpu.PrefetchScalarGridSpec(
        num_scalar_prefetch=1,
        grid=(B, nb),
        in_specs=[pl.BlockSpec((L, q_width), lambda b, n, s: (cur(b, n, s), 0)),
                  pl.BlockSpec((L, LANES), lambda b, n, s: (cur(b, n, s), k_blk)),
                  pl.BlockSpec((L, LANES), lambda b, n, s: (prev(b, n, s), k_blk)),
                  pl.BlockSpec((L, LANES), lambda b, n, s: (cur(b, n, s), v_blk)),
                  pl.BlockSpec((L, LANES), lambda b, n, s: (prev(b, n, s), v_blk)),
                  pl.BlockSpec((L, 1), lambda b, n, s: (cur(b, n, s), 0)),
                  pl.BlockSpec((L, 1), lambda b, n, s: (prev(b, n, s), 0)),
                  pl.BlockSpec((1, LANES), lambda b, n, s: (0, 0))],
        out_specs=pl.BlockSpec((L, q_width), lambda b, n, s: (cur(b, n, s), 0)),
    )
    return pl.pallas_call(
        _attn_body,
        grid_spec=grid_spec,
        out_shape=jax.ShapeDtypeStruct((B * S, q_width), BF16),
        compiler_params=_cparams("arbitrary", "arbitrary"),
        name="swa_attn",
    )(sinks, proj, proj, proj, proj, proj, pos_f, pos_f, inv_lane)


def _log_sigmoid(z):
    return jnp.minimum(z, 0.0) - jnp.log1p(jnp.exp(-jnp.abs(z)))


def _mlstm_body(qk_ref, v_ref, og_ref, gates_ref, cw_ref, gn_ref, out_ref, hist_ref, c_ref, m_ref):
    L = MLSTM_CHUNK
    H = MLSTM_HEADS
    dqk = qk_ref.shape[1] // (2 * H)
    dv = v_ref.shape[1] // H
    tail = SUBLANES

    @pl.when(pl.program_id(1) == 0)
    def _():
        hist_ref[0:tail, :] = jnp.zeros((tail, hist_ref.shape[1]), F32)
        c_ref[...] = jnp.zeros(c_ref.shape, F32)
        m_ref[...] = jnp.zeros(m_ref.shape, F32)

    hist_ref[tail:tail + L, :] = qk_ref[...].astype(F32)
    conv = hist_ref[tail:tail + L, :] * cw_ref[CONV_WIDTH - 1:CONV_WIDTH, :]
    for t in range(CONV_WIDTH - 1):
        off = tail - (CONV_WIDTH - 1) + t
        conv = conv + hist_ref[off:off + L, :] * cw_ref[t:t + 1, :]
    hist_ref[0:tail, :] = hist_ref[L:L + tail, :]
    qk = conv * _sigmoid(conv)

    gates = gates_ref[...]
    capped = GATE_SOFTCAP * jnp.tanh(gates / GATE_SOFTCAP)
    log_f = _log_sigmoid(capped)
    row = lax.broadcasted_iota(I32, (L, L), 0)
    col = lax.broadcasted_iota(I32, (L, L), 1)
    causal = col <= row
    b_all = jnp.dot(causal.astype(F32), log_f, preferred_element_type=F32, precision=HIGHEST)
    b_all_t = b_all.T
    capped_t = capped.T
    ones_col = jnp.where(lax.broadcasted_iota(I32, (L, LANES), 1) == 0, 1.0, 0.0).astype(BF16)

    for h in range(H):
        b_col = b_all[:, H + h:H + h + 1]
        li_col = capped[:, h:h + 1]
        b_row = b_all_t[H + h:H + h + 1, :]
        li_row = capped_t[h:h + 1, :]
        g_last = b_all[L - 1:L, H + h:H + h + 1]
        m_prev = m_ref[h:h + 1, 0:1]

        d_log = jnp.where(causal, b_col - b_row + li_row, NEG_INF)
        inter = b_col + m_prev
        m = jnp.maximum(inter, jnp.max(d_log, axis=-1, keepdims=True))
        q_h = qk[:, h * dqk:(h + 1) * dqk].astype(BF16)
        k_f = qk[:, (H + h) * dqk:(H + h + 1) * dqk] * (dqk ** -0.5)
        s_qk = lax.dot_general(q_h, k_f.astype(BF16), (((1,), (1,)), ((), ())),
                               preferred_element_type=F32) * jnp.exp(d_log - m)
        v_ext = jnp.concatenate([v_ref[:, h * dv:(h + 1) * dv], ones_col], axis=1)
        c_prev = c_ref[h]
        tot = (jnp.dot(s_qk.astype(BF16), v_ext, preferred_element_type=F32)
               + jnp.exp(inter - m) * jnp.dot(q_h, c_prev.astype(BF16), preferred_element_type=F32))
        h_t = tot[:, :dv] / jnp.maximum(jnp.abs(tot[:, dv:dv + 1]), jnp.exp(-m))

        a_col = g_last - b_col + li_col
        m_new = jnp.maximum(g_last + m_prev, jnp.max(a_col, axis=0, keepdims=True))
        wk = jnp.exp(a_col - m_new) * k_f
        kv = jnp.dot(wk.T.astype(BF16), v_ext, preferred_element_type=F32)
        c_ref[h] = jnp.exp(g_last + m_prev - m_new) * c_prev + kv
        m_ref[h:h + 1, :] = jnp.broadcast_to(m_new, (1, LANES))

        hn = _rms(h_t, gn_ref[:, h * dv:(h + 1) * dv])
        out_ref[:, h * dv:(h + 1) * dv] = (_sigmoid(og_ref[:, h * dv:(h + 1) * dv].astype(F32)) * hn).astype(BF16)


def _mlstm(proj, gates, conv_w, g_norm, B, S, qk_blk, v_blk, og_blk, qk_width, v_width):
    L = MLSTM_CHUNK
    nc = S // L
    dv = v_width // MLSTM_HEADS
    dqk = qk_width // (2 * MLSTM_HEADS)

    def rows(b, c):
        return b * nc + c

    return pl.pallas_call(
        _mlstm_body,
        grid=(B, nc),
        in_specs=[pl.BlockSpec((L, qk_width), lambda b, c: (rows(b, c), qk_blk)),
                  pl.BlockSpec((L, v_width), lambda b, c: (rows(b, c), v_blk)),
                  pl.BlockSpec((L, v_width), lambda b, c: (rows(b, c), og_blk)),
                  pl.BlockSpec((L, LANES), lambda b, c: (rows(b, c), 0)),
                  pl.BlockSpec((CONV_WIDTH, qk_width), lambda b, c: (0, 0)),
                  pl.BlockSpec((1, v_width), lambda b, c: (0, 0))],
        out_specs=pl.BlockSpec((L, v_width), lambda b, c: (rows(b, c), 0)),
        out_shape=jax.ShapeDtypeStruct((B * S, v_width), BF16),
        scratch_shapes=[pltpu.VMEM((L + 2 * SUBLANES, qk_width), F32),
                        pltpu.VMEM((MLSTM_HEADS, dqk, dv + LANES), F32),
                        pltpu.VMEM((SUBLANES, LANES), F32)],
        compiler_params=_cparams("arbitrary", "arbitrary"),
        name="mlstm",
    )(proj, proj, proj, gates, conv_w, g_norm)


def _mix_body(a_ref, m_ref, ga_ref, gm_ref, x_ref, mod_ref, gpost_ref, gpre_ref,
              wa_ref, wm_ref, wo_ref, wr_ref, br_ref, x1_ref, h2_ref, lg_ref):
    ya = jnp.dot(a_ref[...], wa_ref[...], preferred_element_type=F32)
    ym = jnp.dot(m_ref[...], wm_ref[...], preferred_element_type=F32)
    merged = _sigmoid(ga_ref[...].astype(F32)) * ya + _sigmoid(gm_ref[...].astype(F32)) * ym
    y = jnp.dot(merged.astype(BF16), wo_ref[...], preferred_element_type=F32)
    x1 = x_ref[...] + mod_ref[0, 2:3, :] * _rms(y, gpost_ref[...])
    x1_ref[...] = x1
    h2 = _rms(x1, gpre_ref[...]) * (1.0 + mod_ref[0, 4:5, :]) + mod_ref[0, 3:4, :]
    h2_ref[...] = h2
    lg_ref[...] = jnp.dot(h2, wr_ref[...], preferred_element_type=F32, precision=HIGHEST) + br_ref[...]


def _mix_out(attn_o, mls_o, proj, x2, mod3, g_post, g_pre, wa, wm, wo, wr, br, rows_per_batch, ga_blk, gm_blk, tm=256):
    N, D = x2.shape
    tpb = rows_per_batch // tm
    wa_w, wm_w = attn_o.shape[1], mls_o.shape[1]
    const = lambda i: (0, 0)
    return pl.pallas_call(
        _mix_body,
        grid=(N // tm,),
        in_specs=[pl.BlockSpec((tm, wa_w), lambda i: (i, 0)),
                  pl.BlockSpec((tm, wm_w), lambda i: (i, 0)),
                  pl.BlockSpec((tm, D), lambda i: (i, ga_blk)),
                  pl.BlockSpec((tm, D), lambda i: (i, gm_blk)),
                  pl.BlockSpec((tm, D), lambda i: (i, 0)),
                  pl.BlockSpec((1, 6, D), lambda i: (i // tpb, 0, 0)),
                  pl.BlockSpec((1, D), const),
                  pl.BlockSpec((1, D), const),
                  pl.BlockSpec((wa_w, D), const),
                  pl.BlockSpec((wm_w, D), const),
                  pl.BlockSpec((D, D), const),
                  pl.BlockSpec((D, LANES), const),
                  pl.BlockSpec((1, LANES), const)],
        out_specs=[pl.BlockSpec((tm, D), lambda i: (i, 0)),
                   pl.BlockSpec((tm, D), lambda i: (i, 0)),
                   pl.BlockSpec((tm, LANES), lambda i: (i, 0))],
        out_shape=[jax.ShapeDtypeStruct((N, D), F32),
                   jax.ShapeDtypeStruct((N, D), F32),
                   jax.ShapeDtypeStruct((N, LANES), F32)],
        compiler_params=_cparams("arbitrary"),
        name="mix_out",
    )(attn_o, mls_o, proj, proj, x2, mod3, g_post, g_pre, wa, wm, wo, wr, br)


def _route_body(lg_ref, e_ref, w_ref, r_ref, cnt_ref, run_ref):
    T = lg_ref.shape[0]
    E = N_EXPERTS

    @pl.when(pl.program_id(0) == 0)
    def _():
        run_ref[...] = jnp.zeros(run_ref.shape, F32)

    lg = lg_ref[...].T[0:E, :]
    eidx = lax.broadcasted_iota(I32, (E, T), 0)
    tops, idxs, hots = [], [], []
    for _ in range(TOP_K):
        mx = jnp.max(lg, axis=0, keepdims=True)
        ik = jnp.min(jnp.where(lg == mx, eidx, E), axis=0, keepdims=True)
        hot = eidx == ik
        lg = jnp.where(hot, NEG_INF, lg)
        tops.append(mx)
        idxs.append(ik)
        hots.append(hot)
    exps = [jnp.exp(t - tops[0]) for t in tops]
    tot = exps[0] + exps[1] + exps[2] + exps[3]

    sel = jnp.zeros((E, T), F32)
    for hot in hots:
        sel = sel + jnp.where(hot, 1.0, 0.0)
    before = lax.broadcasted_iota(I32, (T, T), 0) < lax.broadcasted_iota(I32, (T, T), 1)
    excl = jnp.dot(sel.astype(BF16), jnp.where(before, 1.0, 0.0).astype(BF16), preferred_element_type=F32)
    base = excl + run_ref[:, 0:1]
    for k in range(TOP_K):
        e_ref[k:k + 1, :] = idxs[k]
        w_ref[k:k + 1, :] = exps[k] / tot
        r_ref[k:k + 1, :] = jnp.sum(jnp.where(hots[k], base, 0.0), axis=0, keepdims=True).astype(I32)
    run_ref[...] = run_ref[...] + jnp.sum(sel, axis=1, keepdims=True)
    cnt_ref[...] = run_ref[...]


def _route(logits, T=512):
    N = logits.shape[0]
    return pl.pallas_call(
        _route_body,
        grid=(N // T,),
        in_specs=[pl.BlockSpec((T, LANES), lambda i: (i, 0))],
        out_specs=[pl.BlockSpec((TOP_K, T), lambda i: (0, i)),
                   pl.BlockSpec((TOP_K, T), lambda i: (0, i)),
                   pl.BlockSpec((TOP_K, T), lambda i: (0, i)),
                   pl.BlockSpec((N_EXPERTS, LANES), lambda i: (0, 0))],
        out_shape=[jax.ShapeDtypeStruct((TOP_K, N), I32),
                   jax.ShapeDtypeStruct((TOP_K, N), F32),
                   jax.ShapeDtypeStruct((TOP_K, N), I32),
                   jax.ShapeDtypeStruct((N_EXPERTS, LANES), F32)],
        scratch_shapes=[pltpu.VMEM((N_EXPERTS, LANES), F32)],
        compiler_params=_cparams("arbitrary"),
        name="route",
    )(logits)


def _dispatch_body(dest_ref, h_hbm, xs_in, xs_hbm, sem):
    del xs_in
    tc = dest_ref.shape[1]
    base = pl.program_id(0) * tc

    def row_copy(tok, d):
        return pltpu.make_async_copy(h_hbm.at[pl.ds(tok, 1), :], xs_hbm.at[pl.ds(d, 1), :], sem)

    def issue(t, carry):
        for k in range(TOP_K):
            row_copy(base + t, dest_ref[k, t]).start()
        return carry

    def drain(t, carry):
        for k in range(TOP_K):
            row_copy(base + t, dest_ref[k, t]).wait()
        return carry

    lax.fori_loop(0, tc, issue, 0)
    lax.fori_loop(0, tc, drain, 0)


def _dispatch(dest, h2, n_rows, tc=512):
    N, D = h2.shape
    xs0 = jnp.zeros((n_rows, D), h2.dtype)
    return pl.pallas_call(
        _dispatch_body,
        grid=(N // tc,),
        in_specs=[pl.BlockSpec((TOP_K, tc), lambda i: (0, i), memory_space=pltpu.SMEM),
                  pl.BlockSpec(memory_space=pl.ANY),
                  pl.BlockSpec(memory_space=pl.ANY)],
        out_specs=pl.BlockSpec(memory_space=pl.ANY),
        out_shape=jax.ShapeDtypeStruct((n_rows, D), h2.dtype),
        scratch_shapes=[pltpu.SemaphoreType.DMA(())],
        input_output_aliases={2: 0},
        compiler_params=_cparams("arbitrary"),
        name="dispatch",
    )(dest, h2, xs0)


def _ffn_up_body(te_ref, x_ref, wg_ref, wl_ref, bg_ref, bl_ref, o_ref):
    del te_ref
    xb = x_ref[...].astype(BF16)
    g = jnp.dot(xb, wg_ref[0], preferred_element_type=F32) + bg_ref[0]
    l = jnp.dot(xb, wl_ref[0], preferred_element_type=F32) + bl_ref[0]
    glu = jnp.minimum(g, SWIGLU_LIMIT)
    lin = jnp.clip(l, -SWIGLU_LIMIT, SWIGLU_LIMIT)
    o_ref[...] = (glu * _sigmoid(SWIGLU_ALPHA * glu) * (lin + 1.0)).astype(BF16)


def _ffn_up(tile_e, xs, wg, wl, bg, bl, tm, tf=1024):
    P, D = xs.shape
    F = wg.shape[2]
    grid_spec = pltpu.PrefetchScalarGridSpec(
        num_scalar_prefetch=1,
        grid=(F // tf, P // tm),
        in_specs=[pl.BlockSpec((tm, D), lambda j, i, te: (i, 0)),
                  pl.BlockSpec((1, D, tf), lambda j, i, te: (te[i], 0, j)),
                  pl.BlockSpec((1, D, tf), lambda j, i, te: (te[i], 0, j)),
                  pl.BlockSpec((1, 1, tf), lambda j, i, te: (te[i], 0, j)),
                  pl.BlockSpec((1, 1, tf), lambda j, i, te: (te[i], 0, j))],
        out_specs=pl.BlockSpec((tm, tf), lambda j, i, te: (i, j)),
    )
    return pl.pallas_call(
        _ffn_up_body,
        grid_spec=grid_spec,
        out_shape=jax.ShapeDtypeStruct((P, F), BF16),
        compiler_params=_cparams("arbitrary", "arbitrary"),
        name="ffn_up",
    )(tile_e, xs, wg, wl, bg, bl)


def _ffn_down_body(te_ref, a_ref, w_ref, b_ref, o_ref):
    del te_ref
    o_ref[...] = jnp.dot(a_ref[...], w_ref[0], preferred_element_type=F32) + b_ref[0]


def _ffn_down(tile_e, act, wd, bd, tm):
    P, F = act.shape
    D = wd.shape[2]
    grid_spec = pltpu.PrefetchScalarGridSpec(
        num_scalar_prefetch=1,
        grid=(P // tm,),
        in_specs=[pl.BlockSpec((tm, F), lambda i, te: (i, 0)),
                  pl.BlockSpec((1, F, D), lambda i, te: (te[i], 0, 0)),
                  pl.BlockSpec((1, 1, D), lambda i, te: (te[i], 0, 0))],
        out_specs=pl.BlockSpec((tm, D), lambda i, te: (i, 0)),
    )
    return pl.pallas_call(
        _ffn_down_body,
        grid_spec=grid_spec,
        out_shape=jax.ShapeDtypeStruct((P, D), F32),
        compiler_params=_cparams("arbitrary"),
        name="ffn_down",
    )(tile_e, act, wd, bd)


def _combine_body(dcur_ref, dnxt_ref, w_ref, x1_ref, mod_ref, g_ref, y_hbm, o_ref, buf_ref, sem_ref):
    i = pl.program_id(0)
    n = pl.num_programs(0)
    tt = x1_ref.shape[0]
    slot = i % 2

    def row_copy(d_ref, s, t, k):
        return pltpu.make_async_copy(y_hbm.at[pl.ds(d_ref[k, t], 1), :],
                                     buf_ref.at[s, k, pl.ds(t, 1), :], sem_ref.at[s])

    def issue(d_ref, s):
        def body(t, carry):
            for k in range(TOP_K):
                row_copy(d_ref, s, t, k).start()
            return carry
        lax.fori_loop(0, tt, body, 0)

    @pl.when(i == 0)
    def _():
        issue(dcur_ref, 0)

    @pl.when(i + 1 < n)
    def _():
        issue(dnxt_ref, 1 - slot)

    def drain(t, carry):
        for k in range(TOP_K):
            row_copy(dcur_ref, slot, t, k).wait()
        return carry
    lax.fori_loop(0, tt, drain, 0)

    acc = jnp.zeros(x1_ref.shape, F32)
    for k in range(TOP_K):
        acc = acc + w_ref[:, k:k + 1] * buf_ref[slot, k]
    o_ref[...] = x1_ref[...] + mod_ref[0, 5:6, :] * _rms(acc, g_ref[...])


def _combine(dest, top_w_t, x1, mod3, g_post, y_rows, rows_per_batch, tt=128):
    N, D = x1.shape
    nt = N // tt
    tpb = rows_per_batch // tt
    return pl.pallas_call(
        _combine_body,
        grid=(nt,),
        in_specs=[pl.BlockSpec((TOP_K, tt), lambda i: (0, i), memory_space=pltpu.SMEM),
                  pl.BlockSpec((TOP_K, tt), lambda i: (0, jnp.minimum(i + 1, nt - 1)), memory_space=pltpu.SMEM),
                  pl.BlockSpec((tt, TOP_K), lambda i: (i, 0)),
                  pl.BlockSpec((tt, D), lambda i: (i, 0)),
                  pl.BlockSpec((1, 6, D), lambda i: (i // tpb, 0, 0)),
                  pl.BlockSpec((1, D), lambda i: (0, 0)),
                  pl.BlockSpec(memory_space=pl.ANY)],
        out_specs=pl.BlockSpec((tt, D), lambda i: (i, 0)),
        out_shape=jax.ShapeDtypeStruct((N, D), F32),
        scratch_shapes=[pltpu.VMEM((2, TOP_K, tt, D), F32),
                        pltpu.SemaphoreType.DMA((2,))],
        compiler_params=_cparams("arbitrary"),
        name="combine",
    )(dest, dest, top_w_t, x1, mod3, g_post, y_rows)


def _block(x, mod3, positions, g_pre_mix, g_post_mix, g_pre_ffn, g_post_ffn, w_in, b_in, conv_qk, sinks, g_mlstm,
           w_attn_proj, w_mlstm_proj, w_out, w_router, b_router, w_gate_up, b_gate_up, w_down, b_down, ffn_tm=512):
    B, S, D = x.shape
    N = B * S
    aq_w = w_attn_proj.shape[0]
    akv_w = (aq_w // ATTN_HEAD_DIM // ATTN_GROUP) * ATTN_HEAD_DIM
    mv_w = w_mlstm_proj.shape[0]
    mqk_w = mv_w // 2
    H = MLSTM_HEADS
    sizes = (aq_w, akv_w, akv_w, mqk_w, mqk_w, mv_w, mv_w, H, H, D, D)
    offs = [0]
    for sz in sizes:
        offs.append(offs[-1] + sz)
    seg_w = lambda k: w_in[:, offs[k]:offs[k + 1]]
    seg_b = lambda k: b_in[offs[k]:offs[k + 1]]
    order = (0, 3, 4, 5, 6, 9, 10, 1, 2)
    w_main = jnp.concatenate([seg_w(k) for k in order], axis=1).astype(BF16)
    b_main = jnp.concatenate([seg_b(k) for k in order]).reshape(1, -1)
    pad = LANES - 2 * H
    w_gates = jnp.concatenate([seg_w(7), seg_w(8), jnp.zeros((D, pad), F32)], axis=1).astype(BF16)
    b_gates = jnp.concatenate([seg_b(7), seg_b(8), jnp.zeros((pad,), F32)]).reshape(1, LANES)

    x2 = x.reshape(N, D)
    proj, gates = _in_proj(x2, mod3, g_pre_mix.reshape(1, D), w_main, b_main, w_gates, b_gates, S)

    c_mq = aq_w
    c_mv = aq_w + 2 * mqk_w
    c_mo = c_mv + mv_w
    c_ga = c_mo + mv_w
    c_gm = c_ga + D
    c_ak = c_gm + D
    c_av = c_ak + akv_w

    half = ROPE_DIM // 2
    inv_freq = ROPE_THETA ** (-jnp.arange(half, dtype=F32) / half)
    lane_d = jnp.arange(LANES) % ATTN_HEAD_DIM
    inv_lane = jnp.where(lane_d < ROPE_DIM, inv_freq[lane_d % half], 0.0).reshape(1, LANES).astype(F32)
    pos_f = positions.astype(F32).reshape(N, 1)
    attn_o = _swa_attention(proj, pos_f, inv_lane, sinks, B, S, aq_w, c_ak // LANES, c_av // LANES)

    mls_o = _mlstm(proj, gates, conv_qk, g_mlstm.reshape(1, mv_w), B, S,
                   c_mq // (2 * mqk_w), c_mv // mv_w, c_mo // mv_w, 2 * mqk_w, mv_w)

    E = N_EXPERTS
    wr = jnp.concatenate([w_router, jnp.zeros((D, LANES - E), F32)], axis=1)
    br = jnp.concatenate([b_router, jnp.full((LANES - E,), -1e30, F32)]).reshape(1, LANES)
    x1, h2, logits = _mix_out(attn_o, mls_o, proj, x2, mod3, g_post_mix.reshape(1, D), g_pre_ffn.reshape(1, D),
                              w_attn_proj.astype(BF16), w_mlstm_proj.astype(BF16), w_out.astype(BF16), wr, br,
                              S, c_ga // D, c_gm // D)

    top_e, top_w, rank, counts = _route(logits)
    cnt = counts[:, 0].astype(I32)
    padded = (cnt + ffn_tm - 1) // ffn_tm * ffn_tm
    pad_ends = jnp.cumsum(padded)
    pad_starts = pad_ends - padded
    n_rows = N * TOP_K + E * ffn_tm
    n_tiles = n_rows // ffn_tm
    tile_e = jnp.minimum(jnp.searchsorted(pad_ends, jnp.arange(n_tiles, dtype=I32) * ffn_tm, side="right"),
                         E - 1).astype(I32)
    dest = pad_starts[top_e] + rank

    xs = _dispatch(dest, h2, n_rows)
    F = w_down.shape[1]
    wg = w_gate_up[:, :, 0::2].astype(BF16)
    wl = w_gate_up[:, :, 1::2].astype(BF16)
    bg = b_gate_up[:, 0::2].reshape(E, 1, F)
    bl = b_gate_up[:, 1::2].reshape(E, 1, F)
    act = _ffn_up(tile_e, xs, wg, wl, bg, bl, ffn_tm)
    y_rows = _ffn_down(tile_e, act, w_down.astype(BF16), b_down.reshape(E, 1, D), ffn_tm)
    out = _combine(dest, top_w.T, x1, mod3, g_post_ffn.reshape(1, D), y_rows, S)
    return out.reshape(B, S, D)


def kernel(x, c, positions, w_ada, b_ada, g_pre_mix, g_post_mix, g_pre_ffn, g_post_ffn, w_in, b_in, conv_qk, sinks,
           g_mlstm, w_attn_proj, w_mlstm_proj, w_out, w_router, b_router, w_gate_up, b_gate_up, w_down, b_down):
    B, S, D = x.shape
    depth = w_ada.shape[0]
    for l in range(depth):
        mod3 = _ada_ln(c, w_ada[l], b_ada[l]).reshape(B, 6, D)
        x = _block(x, mod3, positions, g_pre_mix[l], g_post_mix[l], g_pre_ffn[l], g_post_ffn[l], w_in[l], b_in[l],
                   conv_qk[l], sinks[l], g_mlstm[l], w_attn_proj[l], w_mlstm_proj[l], w_out[l], w_router[l],
                   b_router[l], w_gate_up[l], b_gate_up[l], w_down[l], b_down[l])
    return x
```

```python
import functools

import jax
import jax.numpy as jnp
from jax import lax
from jax.experimental import pallas as pl
from jax.experimental.pallas import tpu as pltpu

F32 = jnp.float32
BF16 = jnp.bfloat16
I32 = jnp.int32
HIGHEST = lax.Precision.HIGHEST

ATTN_HEAD_DIM = 64
ATTN_GROUP = 8
WINDOW = 128
ATTN_BLOCK = 128
ROPE_THETA = 500000.0
ROPE_DIM = ATTN_HEAD_DIM // 4
MLSTM_HEADS = 4
MLSTM_CHUNK = 128
CONV_WIDTH = 4
GATE_SOFTCAP = 15.0
N_EXPERTS = 32
TOP_K = 4
SWIGLU_ALPHA = 1.702
SWIGLU_LIMIT = 7.0
NORM_EPS = 1e-6

LANES = 128
SUBLANES = 8
VMEM_LIMIT = 56 * 1024 * 1024

NEG_INF = float("-inf")


def _cparams(*sem):
    return pltpu.CompilerParams(dimension_semantics=sem, vmem_limit_bytes=VMEM_LIMIT)


def _rms(x, g):
    ms = jnp.mean(x * x, axis=-1, keepdims=True)
    return x * lax.rsqrt(ms + NORM_EPS) * g


def _sigmoid(x):
    return 1.0 / (1.0 + jnp.exp(-x))


def _ada_body(c_ref, w_ref, b_ref, o_ref):
    c = c_ref[...]
    cond = c * _sigmoid(c)
    o_ref[...] = jnp.dot(cond, w_ref[...], preferred_element_type=F32, precision=HIGHEST) + b_ref[...]


def _ada_ln(c, w_ada, b_ada):
    B, D = c.shape
    W = w_ada.shape[1]
    tn = 1024
    cp = jnp.zeros((SUBLANES, D), F32).at[:B].set(c)
    out = pl.pallas_call(
        _ada_body,
        grid=(W // tn,),
        in_specs=[pl.BlockSpec((SUBLANES, D), lambda j: (0, 0)),
                  pl.BlockSpec((D, tn), lambda j: (0, j)),
                  pl.BlockSpec((1, tn), lambda j: (0, j))],
        out_specs=pl.BlockSpec((SUBLANES, tn), lambda j: (0, j)),
        out_shape=jax.ShapeDtypeStruct((SUBLANES, W), F32),
        compiler_params=_cparams("arbitrary"),
        name="ada_ln",
    )(cp, w_ada, b_ada.reshape(1, W))
    return out[:B]


def _inproj_body(x_ref, mod_ref, g_ref, w_ref, b_ref, wg_ref, bg_ref, o_ref, og_ref, h_ref):
    @pl.when(pl.program_id(1) == 0)
    def _():
        h = _rms(x_ref[...], g_ref[...]) * (1.0 + mod_ref[0, 1:2, :]) + mod_ref[0, 0:1, :]
        hb = h.astype(BF16)
        h_ref[...] = hb
        og_ref[...] = jnp.dot(hb, wg_ref[...], preferred_element_type=F32) + bg_ref[...]

    acc = jnp.dot(h_ref[...], w_ref[...], preferred_element_type=F32) + b_ref[...]
    o_ref[...] = acc.astype(BF16)


def _in_proj(x2, mod3, g, w_main, b_main, w_gates, b_gates, rows_per_batch, tm=1024, tn=768):
    N, D = x2.shape
    W = w_main.shape[1]
    tpb = rows_per_batch // tm
    return pl.pallas_call(
        _inproj_body,
        grid=(N // tm, W // tn),
        in_specs=[pl.BlockSpec((tm, D), lambda i, j: (i, 0)),
                  pl.BlockSpec((1, 6, D), lambda i, j: (i // tpb, 0, 0)),
                  pl.BlockSpec((1, D), lambda i, j: (0, 0)),
                  pl.BlockSpec((D, tn), lambda i, j: (0, j)),
                  pl.BlockSpec((1, tn), lambda i, j: (0, j)),
                  pl.BlockSpec((D, LANES), lambda i, j: (0, 0)),
                  pl.BlockSpec((1, LANES), lambda i, j: (0, 0))],
        out_specs=[pl.BlockSpec((tm, tn), lambda i, j: (i, j)),
                   pl.BlockSpec((tm, LANES), lambda i, j: (i, 0))],
        out_shape=[jax.ShapeDtypeStruct((N, W), BF16),
                   jax.ShapeDtypeStruct((N, LANES), F32)],
        scratch_shapes=[pltpu.VMEM((tm, D), BF16)],
        compiler_params=_cparams("arbitrary", "arbitrary"),
        name="in_proj",
    )(x2, mod3, g, w_main, b_main, w_gates, b_gates)


def _attn_body(sinks_ref, q_ref, kc_ref, kp_ref, vc_ref, vp_ref, posc_ref, posp_ref, inv_ref, o_ref):
    n = pl.program_id(1)
    L = ATTN_BLOCK
    inv = inv_ref[...]
    lane = lax.broadcasted_iota(I32, (1, LANES), 1)
    first_half = (lane % ATTN_HEAD_DIM) < (ROPE_DIM // 2)
    low_lanes = lane < ATTN_HEAD_DIM

    ang_c = posc_ref[...] * inv
    ang_p = posp_ref[...] * inv
    cos_c, sin_c = jnp.cos(ang_c), jnp.sin(ang_c)
    cos_p, sin_p = jnp.cos(ang_p), jnp.sin(ang_p)

    def rope(x, cs, sn):
        swapped = jnp.where(first_half, -pltpu.roll(x, LANES - ROPE_DIM // 2, 1), pltpu.roll(x, ROPE_DIM // 2, 1))
        return x * cs + swapped * sn

    k2 = jnp.concatenate([rope(kp_ref[...].astype(F32), cos_p, sin_p),
                          rope(kc_ref[...].astype(F32), cos_c, sin_c)], axis=0)
    v2 = jnp.concatenate([vp_ref[...], vc_ref[...]], axis=0).astype(F32)
    k2s = pltpu.roll(k2, ATTN_HEAD_DIM, 1)
    v2s = pltpu.roll(v2, ATTN_HEAD_DIM, 1)
    zero = jnp.zeros_like(k2)
    k_lo = [jnp.where(low_lanes, k2, zero).astype(BF16), jnp.where(low_lanes, k2s, zero).astype(BF16)]
    k_hi = [jnp.where(low_lanes, zero, k2s).astype(BF16), jnp.where(low_lanes, zero, k2).astype(BF16)]
    v_lo = [jnp.where(low_lanes, v2, zero).astype(BF16), jnp.where(low_lanes, v2s, zero).astype(BF16)]
    v_hi = [jnp.where(low_lanes, zero, v2s).astype(BF16), jnp.where(low_lanes, zero, v2).astype(BF16)]

    qi = lax.broadcasted_iota(I32, (L, 2 * L), 0)
    kj = lax.broadcasted_iota(I32, (L, 2 * L), 1)
    diff = qi + L - kj
    valid = (diff >= 0) & (diff < WINDOW) & ((kj >= L) | (n > 0))

    scale = ATTN_HEAD_DIM ** -0.5
    n_pairs = q_ref.shape[1] // LANES
    for j in range(n_pairs):
        hk = (2 * j) // ATTN_GROUP
        qp = (rope(q_ref[:, j * LANES:(j + 1) * LANES].astype(F32), cos_c, sin_c) * scale).astype(BF16)
        acc = jnp.zeros((L, LANES), F32)
        for half, (kx, vx) in enumerate(((k_lo[hk], v_lo[hk]), (k_hi[hk], v_hi[hk]))):
            s = lax.dot_general(qp, kx, (((1,), (1,)), ((), ())), preferred_element_type=F32)
            s = jnp.where(valid, s, NEG_INF)
            sink = sinks_ref[2 * j + half]
            m = jnp.maximum(jnp.max(s, axis=-1, keepdims=True), sink)
            p = jnp.exp(s - m)
            denom = jnp.sum(p, axis=-1, keepdims=True) + jnp.exp(sink - m)
            acc = acc + jnp.dot(p.astype(BF16), vx, preferred_element_type=F32) / denom
        o_ref[:, j * LANES:(j + 1) * LANES] = acc.astype(BF16)


def _swa_attention(proj, pos_f, inv_lane, sinks, B, S, q_width, k_blk, v_blk):
    L = ATTN_BLOCK
    nb = S // L

    def cur(b, n, s):
        return b * nb + n

    def prev(b, n, s):
        return b * nb + jnp.maximum(n - 1, 0)

    grid_spec = pltpu.PrefetchScalarGridSpec(
        num_scalar_prefetch=1,
        grid=(B, nb),
        in_specs=[pl.BlockSpec((L, q_width), lambda b, n, s: (cur(b, n, s), 0)),
                  pl.BlockSpec((L, LANES), lambda b, n, s: (cur(b, n, s), k_blk)),
                  pl.BlockSpec((L, LANES), lambda b, n, s: (prev(b, n, s), k_blk)),
                  pl.BlockSpec((L, LANES), lambda b, n, s: (cur(b, n, s), v_blk)),
                  pl.BlockSpec((L, LANES), lambda b, n, s: (prev(b, n, s), v_blk)),
                  pl.BlockSpec((L, 1), lambda b, n, s: (cur(b, n, s), 0)),
                  pl.BlockSpec((L, 1), lambda b, n, s: (prev(b, n, s), 0)),
                  pl.BlockSpec((1, LANES), lambda b, n, s: (0, 0))],
        out_specs=pl.BlockSpec((L, q_width), lambda b, n, s: (cur(b, n, s), 0)),
    )
    return pl.pallas_call(
        _attn_body,
        grid_spec=grid_spec,
        out_shape=jax.ShapeDtypeStruct((B * S, q_width), BF16),
        compiler_params=_cparams("arbitrary", "arbitrary"),
        name="swa_attn",
    )(sinks, proj, proj, proj, proj, proj, pos_f, pos_f, inv_lane)


def _log_sigmoid(z):
    return jnp.minimum(z, 0.0) - jnp.log1p(jnp.exp(-jnp.abs(z)))


def _mlstm_body(qk_ref, v_ref, og_ref, gates_ref, cw_ref, gn_ref, out_ref, hist_ref, c_ref, m_ref):
    L = MLSTM_CHUNK
    H = MLSTM_HEADS
    dqk = qk_ref.shape[1] // (2 * H)
    dv = v_ref.shape[1] // H
    tail = SUBLANES

    @pl.when(pl.program_id(1) == 0)
    def _():
        hist_ref[0:tail, :] = jnp.zeros((tail, hist_ref.shape[1]), F32)
        c_ref[...] = jnp.zeros(c_ref.shape, F32)
        m_ref[...] = jnp.zeros(m_ref.shape, F32)

    hist_ref[tail:tail + L, :] = qk_ref[...].astype(F32)
    conv = hist_ref[tail:tail + L, :] * cw_ref[CONV_WIDTH - 1:CONV_WIDTH, :]
    for t in range(CONV_WIDTH - 1):
        off = tail - (CONV_WIDTH - 1) + t
        conv = conv + hist_ref[off:off + L, :] * cw_ref[t:t + 1, :]
    hist_ref[0:tail, :] = hist_ref[L:L + tail, :]
    qk = conv * _sigmoid(conv)

    gates = gates_ref[...]
    capped = GATE_SOFTCAP * jnp.tanh(gates / GATE_SOFTCAP)
    log_f = _log_sigmoid(capped)
    row = lax.broadcasted_iota(I32, (L, L), 0)
    col = lax.broadcasted_iota(I32, (L, L), 1)
    causal = col <= row
    b_all = jnp.dot(causal.astype(F32), log_f, preferred_element_type=F32, precision=HIGHEST)
    b_all_t = b_all.T
    capped_t = capped.T
    ones_col = jnp.where(lax.broadcasted_iota(I32, (L, LANES), 1) == 0, 1.0, 0.0).astype(BF16)

    for h in range(H):
        b_col = b_all[:, H + h:H + h + 1]
        li_col = capped[:, h:h + 1]
        b_row = b_all_t[H + h:H + h + 1, :]
        li_row = capped_t[h:h + 1, :]
        g_last = b_all[L - 1:L, H + h:H + h + 1]
        m_prev = m_ref[h:h + 1, 0:1]

        d_log = jnp.where(causal, b_col - b_row + li_row, NEG_INF)
        inter = b_col + m_prev
        m = jnp.maximum(inter, jnp.max(d_log, axis=-1, keepdims=True))
        q_h = qk[:, h * dqk:(h + 1) * dqk].astype(BF16)
        k_f = qk[:, (H + h) * dqk:(H + h + 1) * dqk] * (dqk ** -0.5)
        s_qk = lax.dot_general(q_h, k_f.astype(BF16), (((1,), (1,)), ((), ())),
                               preferred_element_type=F32) * jnp.exp(d_log - m)
        v_ext = jnp.concatenate([v_ref[:, h * dv:(h + 1) * dv], ones_col], axis=1)
        c_prev = c_ref[h]
        tot = (jnp.dot(s_qk.astype(BF16), v_ext, preferred_element_type=F32)
               + jnp.exp(inter - m) * jnp.dot(q_h, c_prev.astype(BF16), preferred_element_type=F32))
        h_t = tot[:, :dv] / jnp.maximum(jnp.abs(tot[:, dv:dv + 1]), jnp.exp(-m))

        a_col = g_last - b_col + li_col
        m_new = jnp.maximum(g_last + m_prev, jnp.max(a_col, axis=0, keepdims=True))
        wk = jnp.exp(a_col - m_new) * k_f
        kv = jnp.dot(wk.T.astype(BF16), v_ext, preferred_element_type=F32)
        c_ref[h] = jnp.exp(g_last + m_prev - m_new) * c_prev + kv
        m_ref[h:h + 1, :] = jnp.broadcast_to(m_new, (1, LANES))

        hn = _rms(h_t, gn_ref[:, h * dv:(h + 1) * dv])
        out_ref[:, h * dv:(h + 1) * dv] = (_sigmoid(og_ref[:, h * dv:(h + 1) * dv].astype(F32)) * hn).astype(BF16)


def _mlstm(proj, gates, conv_w, g_norm, B, S, qk_blk, v_blk, og_blk, qk_width, v_width):
    L = MLSTM_CHUNK
    nc = S // L
    dv = v_width // MLSTM_HEADS
    dqk = qk_width // (2 * MLSTM_HEADS)

    def rows(b, c):
        return b * nc + c

    return pl.pallas_call(
        _mlstm_body,
        grid=(B, nc),
        in_specs=[pl.BlockSpec((L, qk_width), lambda b, c: (rows(b, c), qk_blk)),
                  pl.BlockSpec((L, v_width), lambda b, c: (rows(b, c), v_blk)),
                  pl.BlockSpec((L, v_width), lambda b, c: (rows(b, c), og_blk)),
                  pl.BlockSpec((L, LANES), lambda b, c: (rows(b, c), 0)),
                  pl.BlockSpec((CONV_WIDTH, qk_width), lambda b, c: (0, 0)),
                  pl.BlockSpec((1, v_width), lambda b, c: (0, 0))],
        out_specs=pl.BlockSpec((L, v_width), lambda b, c: (rows(b, c), 0)),
        out_shape=jax.ShapeDtypeStruct((B * S, v_width), BF16),
        scratch_shapes=[pltpu.VMEM((L + 2 * SUBLANES, qk_width), F32),
                        pltpu.VMEM((MLSTM_HEADS, dqk, dv + LANES), F32),
                        pltpu.VMEM((SUBLANES, LANES), F32)],
        compiler_params=_cparams("arbitrary", "arbitrary"),
        name="mlstm",
    )(proj, proj, proj, gates, conv_w, g_norm)


def _mix_body(a_ref, m_ref, ga_ref, gm_ref, x_ref, mod_ref, gpost_ref, gpre_ref,
              wa_ref, wm_ref, wo_ref, wr_ref, br_ref, x1_ref, h2_ref, lg_ref):
    ya = jnp.dot(a_ref[...], wa_ref[...], preferred_element_type=F32)
    ym = jnp.dot(m_ref[...], wm_ref[...], preferred_element_type=F32)
    merged = _sigmoid(ga_ref[...].astype(F32)) * ya + _sigmoid(gm_ref[...].astype(F32)) * ym
    y = jnp.dot(merged.astype(BF16), wo_ref[...], preferred_element_type=F32)
    x1 = x_ref[...] + mod_ref[0, 2:3, :] * _rms(y, gpost_ref[...])
    x1_ref[...] = x1
    h2 = _rms(x1, gpre_ref[...]) * (1.0 + mod_ref[0, 4:5, :]) + mod_ref[0, 3:4, :]
    h2_ref[...] = h2
    lg_ref[...] = jnp.dot(h2, wr_ref[...], preferred_element_type=F32, precision=HIGHEST) + br_ref[...]


def _mix_out(attn_o, mls_o, proj, x2, mod3, g_post, g_pre, wa, wm, wo, wr, br, rows_per_batch, ga_blk, gm_blk, tm=256):
    N, D = x2.shape
    tpb = rows_per_batch // tm
    wa_w, wm_w = attn_o.shape[1], mls_o.shape[1]
    const = lambda i: (0, 0)
    return pl.pallas_call(
        _mix_body,
        grid=(N // tm,),
        in_specs=[pl.BlockSpec((tm, wa_w), lambda i: (i, 0)),
                  pl.BlockSpec((tm, wm_w), lambda i: (i, 0)),
                  pl.BlockSpec((tm, D), lambda i: (i, ga_blk)),
                  pl.BlockSpec((tm, D), lambda i: (i, gm_blk)),
                  pl.BlockSpec((tm, D), lambda i: (i, 0)),
                  pl.BlockSpec((1, 6, D), lambda i: (i // tpb, 0, 0)),
                  pl.BlockSpec((1, D), const),
                  pl.BlockSpec((1, D), const),
                  pl.BlockSpec((wa_w, D), const),
                  pl.BlockSpec((wm_w, D), const),
                  pl.BlockSpec((D, D), const),
                  pl.BlockSpec((D, LANES), const),
                  pl.BlockSpec((1, LANES), const)],
        out_specs=[pl.BlockSpec((tm, D), lambda i: (i, 0)),
                   pl.BlockSpec((tm, D), lambda i: (i, 0)),
                   pl.BlockSpec((tm, LANES), lambda i: (i, 0))],
        out_shape=[jax.ShapeDtypeStruct((N, D), F32),
                   jax.ShapeDtypeStruct((N, D), F32),
                   jax.ShapeDtypeStruct((N, LANES), F32)],
        compiler_params=_cparams("arbitrary"),
        name="mix_out",
    )(attn_o, mls_o, proj, proj, x2, mod3, g_post, g_pre, wa, wm, wo, wr, br)


def _route_body(lg_ref, e_ref, w_ref, r_ref, cnt_ref, run_ref):
    T = lg_ref.shape[0]
    E = N_EXPERTS

    @pl.when(pl.program_id(0) == 0)
    def _():
        run_ref[...] = jnp.zeros(run_ref.shape, F32)

    lg = lg_ref[...].T[0:E, :]
    eidx = lax.broadcasted_iota(I32, (E, T), 0)
    tops, idxs, hots = [], [], []
    for _ in range(TOP_K):
        mx = jnp.max(lg, axis=0, keepdims=True)
        ik = jnp.min(jnp.where(lg == mx, eidx, E), axis=0, keepdims=True)
        hot = eidx == ik
        lg = jnp.where(hot, NEG_INF, lg)
        tops.append(mx)
        idxs.append(ik)
        hots.append(hot)
    exps = [jnp.exp(t - tops[0]) for t in tops]
    tot = exps[0] + exps[1] + exps[2] + exps[3]

    sel = jnp.zeros((E, T), F32)
    for hot in hots:
        sel = sel + jnp.where(hot, 1.0, 0.0)
    before = lax.broadcasted_iota(I32, (T, T), 0) < lax.broadcasted_iota(I32, (T, T), 1)
    excl = jnp.dot(sel.astype(BF16), jnp.where(before, 1.0, 0.0).astype(BF16), preferred_element_type=F32)
    base = excl + run_ref[:, 0:1]
    for k in range(TOP_K):
        e_ref[k:k + 1, :] = idxs[k]
        w_ref[k:k + 1, :] = exps[k] / tot
        r_ref[k:k + 1, :] = jnp.sum(jnp.where(hots[k], base, 0.0), axis=0, keepdims=True).astype(I32)
    run_ref[...] = run_ref[...] + jnp.sum(sel, axis=1, keepdims=True)
    cnt_ref[...] = run_ref[...]


def _route(logits, T=512):
    N = logits.shape[0]
    return pl.pallas_call(
        _route_body,
        grid=(N // T,),
        in_specs=[pl.BlockSpec((T, LANES), lambda i: (i, 0))],
        out_specs=[pl.BlockSpec((TOP_K, T), lambda i: (0, i)),
                   pl.BlockSpec((TOP_K, T), lambda i: (0, i)),
                   pl.BlockSpec((TOP_K, T), lambda i: (0, i)),
                   pl.BlockSpec((N_EXPERTS, LANES), lambda i: (0, 0))],
        out_shape=[jax.ShapeDtypeStruct((TOP_K, N), I32),
                   jax.ShapeDtypeStruct((TOP_K, N), F32),
                   jax.ShapeDtypeStruct((TOP_K, N), I32),
                   jax.ShapeDtypeStruct((N_EXPERTS, LANES), F32)],
        scratch_shapes=[pltpu.VMEM((N_EXPERTS, LANES), F32)],
        compiler_params=_cparams("arbitrary"),
        name="route",
    )(logits)


def _dest_body(e_ref, r_ref, ps_ref, d_ref):
    T = e_ref.shape[1]
    eidx = lax.broadcasted_iota(I32, (N_EXPERTS, T), 0)
    starts = ps_ref[:, 0:1]
    for k in range(TOP_K):
        base = jnp.sum(jnp.where(eidx == e_ref[k:k + 1, :], starts, 0.0), axis=0, keepdims=True)
        d_ref[k:k + 1, :] = r_ref[k:k + 1, :] + base.astype(I32)


def _dest_rows(top_e, rank, pad_starts_f, T=2048):
    N = top_e.shape[1]
    return pl.pallas_call(
        _dest_body,
        grid=(N // T,),
        in_specs=[pl.BlockSpec((TOP_K, T), lambda i: (0, i)),
                  pl.BlockSpec((TOP_K, T), lambda i: (0, i)),
                  pl.BlockSpec((N_EXPERTS, LANES), lambda i: (0, 0))],
        out_specs=pl.BlockSpec((TOP_K, T), lambda i: (0, i)),
        out_shape=jax.ShapeDtypeStruct((TOP_K, N), I32),
        compiler_params=_cparams("arbitrary"),
        name="dest_rows",
    )(top_e, rank, pad_starts_f)


def _dispatch_body(dest_ref, h_ref, xs_in, xs_hbm, sem):
    del xs_in
    tc = h_ref.shape[0]

    def row_copy(t, k):
        return pltpu.make_async_copy(h_ref.at[pl.ds(t, 1), :], xs_hbm.at[pl.ds(dest_ref[k, t], 1), :], sem)

    def issue(t, carry):
        for k in range(TOP_K):
            row_copy(t, k).start()
        return carry

    def drain(t, carry):
        for k in range(TOP_K):
            row_copy(t, k).wait()
        return carry

    lax.fori_loop(0, tc, issue, 0)
    lax.fori_loop(0, tc, drain, 0)


def _dispatch(dest, h2, n_rows, tc=256):
    N, D = h2.shape
    xs0 = jnp.zeros((n_rows, D), h2.dtype)
    return pl.pallas_call(
        _dispatch_body,
        grid=(N // tc,),
        in_specs=[pl.BlockSpec((TOP_K, tc), lambda i: (0, i), memory_space=pltpu.SMEM),
                  pl.BlockSpec((tc, D), lambda i: (i, 0)),
                  pl.BlockSpec(memory_space=pl.ANY)],
        out_specs=pl.BlockSpec(memory_space=pl.ANY),
        out_shape=jax.ShapeDtypeStruct((n_rows, D), h2.dtype),
        scratch_shapes=[pltpu.SemaphoreType.DMA(())],
        input_output_aliases={2: 0},
        compiler_params=_cparams("arbitrary"),
        name="dispatch",
    )(dest, h2, xs0)


PAIR_CHUNK = 2 * LANES


def _ffn_up_body(te_ref, nv_ref, x_ref, w_ref, b_ref, sel_ref, o_ref):
    del te_ref

    @pl.when(pl.program_id(1) < nv_ref[0])
    def _():
        xb = x_ref[...].astype(BF16)
        gu = jnp.dot(xb, w_ref[0].astype(BF16), preferred_element_type=F32) + b_ref[0]
        glu = jnp.minimum(gu, SWIGLU_LIMIT)
        gate = glu * _sigmoid(SWIGLU_ALPHA * glu)
        lin = jnp.clip(gu, -SWIGLU_LIMIT, SWIGLU_LIMIT) + 1.0
        for c in range(gu.shape[1] // PAIR_CHUNK):
            parts = []
            for v in range(PAIR_CHUNK // LANES):
                lo = c * PAIR_CHUNK + v * LANES
                parts.append(gate[:, lo:lo + LANES] * pltpu.roll(lin[:, lo:lo + LANES], LANES - 1, 1))
            prod = jnp.concatenate(parts, axis=1).astype(BF16)
            o_ref[:, c * LANES:(c + 1) * LANES] = jnp.dot(prod, sel_ref[...], preferred_element_type=F32).astype(BF16)

    @pl.when(pl.program_id(1) >= nv_ref[0])
    def _():
        o_ref[...] = jnp.zeros(o_ref.shape, o_ref.dtype)


def _ffn_up(tile_e, n_valid, xs, w_gate_up, b_gate_up, tm, tf=512):
    P, D = xs.shape
    E, _, F2 = w_gate_up.shape
    F = F2 // 2
    sel = (jnp.arange(PAIR_CHUNK)[:, None] == 2 * jnp.arange(LANES)[None, :]).astype(BF16)

    def row(i, nv):
        return jnp.minimum(i, nv[0] - 1)

    grid_spec = pltpu.PrefetchScalarGridSpec(
        num_scalar_prefetch=2,
        grid=(F // tf, P // tm),
        in_specs=[pl.BlockSpec((tm, D), lambda j, i, te, nv: (row(i, nv), 0)),
                  pl.BlockSpec((1, D, 2 * tf), lambda j, i, te, nv: (te[row(i, nv)], 0, j)),
                  pl.BlockSpec((1, 1, 2 * tf), lambda j, i, te, nv: (te[row(i, nv)], 0, j)),
                  pl.BlockSpec((PAIR_CHUNK, LANES), lambda j, i, te, nv: (0, 0))],
        out_specs=pl.BlockSpec((tm, tf), lambda j, i, te, nv: (i, j)),
    )
    return pl.pallas_call(
        _ffn_up_body,
        grid_spec=grid_spec,
        out_shape=jax.ShapeDtypeStruct((P, F), BF16),
        compiler_params=_cparams("arbitrary", "arbitrary"),
        name="ffn_up",
    )(tile_e, n_valid, xs, w_gate_up, b_gate_up.reshape(E, 1, F2), sel)


def _ffn_down_body(te_ref, nv_ref, a_ref, w_ref, b_ref, o_ref):
    del te_ref

    @pl.when(pl.program_id(1) < nv_ref[0])
    def _():
        o_ref[...] = jnp.dot(a_ref[...], w_ref[0].astype(BF16), preferred_element_type=F32) + b_ref[0]

    @pl.when(pl.program_id(1) >= nv_ref[0])
    def _():
        o_ref[...] = jnp.zeros(o_ref.shape, o_ref.dtype)


def _ffn_down(tile_e, n_valid, act, w_down, b_down, tm, tn=1024):
    P, F = act.shape
    E, _, D = w_down.shape

    def row(i, nv):
        return jnp.minimum(i, nv[0] - 1)

    grid_spec = pltpu.PrefetchScalarGridSpec(
        num_scalar_prefetch=2,
        grid=(D // tn, P // tm),
        in_specs=[pl.BlockSpec((tm, F), lambda j, i, te, nv: (row(i, nv), 0)),
                  pl.BlockSpec((1, F, tn), lambda j, i, te, nv: (te[row(i, nv)], 0, j)),
                  pl.BlockSpec((1, 1, tn), lambda j, i, te, nv: (te[row(i, nv)], 0, j))],
        out_specs=pl.BlockSpec((tm, tn), lambda j, i, te, nv: (i, j)),
    )
    return pl.pallas_call(
        _ffn_down_body,
        grid_spec=grid_spec,
        out_shape=jax.ShapeDtypeStruct((P, D), F32),
        compiler_params=_cparams("arbitrary", "arbitrary"),
        name="ffn_down",
    )(tile_e, n_valid, act, w_down, b_down.reshape(E, 1, D))


def _combine_body(dcur_ref, dnxt_ref, w_ref, x1_ref, mod_ref, g_ref, y_hbm, o_ref, buf_ref, sem_ref):
    i = pl.program_id(0)
    n = pl.num_programs(0)
    tt = x1_ref.shape[0]
    slot = i % 2

    def row_copy(d_ref, s, t, k):
        return pltpu.make_async_copy(y_hbm.at[pl.ds(d_ref[k, t], 1), :],
                                     buf_ref.at[s, k, pl.ds(t, 1), :], sem_ref.at[s])

    def issue(d_ref, s):
        def body(t, carry):
            for k in range(TOP_K):
                row_copy(d_ref, s, t, k).start()
            return carry
        lax.fori_loop(0, tt, body, 0)

    @pl.when(i == 0)
    def _():
        issue(dcur_ref, 0)

    @pl.when(i + 1 < n)
    def _():
        issue(dnxt_ref, 1 - slot)

    def drain(t, carry):
        for k in range(TOP_K):
            row_copy(dcur_ref, slot, t, k).wait()
        return carry
    lax.fori_loop(0, tt, drain, 0)

    acc = jnp.zeros(x1_ref.shape, F32)
    for k in range(TOP_K):
        acc = acc + w_ref[:, k:k + 1] * buf_ref[slot, k]
    o_ref[...] = x1_ref[...] + mod_ref[0, 5:6, :] * _rms(acc, g_ref[...])


def _combine(dest, top_w_t, x1, mod3, g_post, y_rows, rows_per_batch, tt=128):
    N, D = x1.shape
    nt = N // tt
    tpb = rows_per_batch // tt
    return pl.pallas_call(
        _combine_body,
        grid=(nt,),
        in_specs=[pl.BlockSpec((TOP_K, tt), lambda i: (0, i), memory_space=pltpu.SMEM),
                  pl.BlockSpec((TOP_K, tt), lambda i: (0, jnp.minimum(i + 1, nt - 1)), memory_space=pltpu.SMEM),
                  pl.BlockSpec((tt, TOP_K), lambda i: (i, 0)),
                  pl.BlockSpec((tt, D), lambda i: (i, 0)),
                  pl.BlockSpec((1, 6, D), lambda i: (i // tpb, 0, 0)),
                  pl.BlockSpec((1, D), lambda i: (0, 0)),
                  pl.BlockSpec(memory_space=pl.ANY)],
        out_specs=pl.BlockSpec((tt, D), lambda i: (i, 0)),
        out_shape=jax.ShapeDtypeStruct((N, D), F32),
        scratch_shapes=[pltpu.VMEM((2, TOP_K, tt, D), F32),
                        pltpu.SemaphoreType.DMA((2,))],
        compiler_params=_cparams("arbitrary"),
        name="combine",
    )(dest, dest, top_w_t, x1, mod3, g_post, y_rows)


def _block(x, mod3, positions, g_pre_mix, g_post_mix, g_pre_ffn, g_post_ffn, w_in, b_in, conv_qk, sinks, g_mlstm,
           w_attn_proj, w_mlstm_proj, w_out, w_router, b_router, w_gate_up, b_gate_up, w_down, b_down, ffn_tm=512):
    B, S, D = x.shape
    N = B * S
    aq_w = w_attn_proj.shape[0]
    akv_w = (aq_w // ATTN_HEAD_DIM // ATTN_GROUP) * ATTN_HEAD_DIM
    mv_w = w_mlstm_proj.shape[0]
    mqk_w = mv_w // 2
    H = MLSTM_HEADS
    sizes = (aq_w, akv_w, akv_w, mqk_w, mqk_w, mv_w, mv_w, H, H, D, D)
    offs = [0]
    for sz in sizes:
        offs.append(offs[-1] + sz)
    seg_w = lambda k: w_in[:, offs[k]:offs[k + 1]]
    seg_b = lambda k: b_in[offs[k]:offs[k + 1]]
    order = (0, 3, 4, 5, 6, 9, 10, 1, 2)
    w_main = jnp.concatenate([seg_w(k) for k in order], axis=1).astype(BF16)
    b_main = jnp.concatenate([seg_b(k) for k in order]).reshape(1, -1)
    pad = LANES - 2 * H
    w_gates = jnp.concatenate([seg_w(7), seg_w(8), jnp.zeros((D, pad), F32)], axis=1).astype(BF16)
    b_gates = jnp.concatenate([seg_b(7), seg_b(8), jnp.zeros((pad,), F32)]).reshape(1, LANES)

    x2 = x.reshape(N, D)
    proj, gates = _in_proj(x2, mod3, g_pre_mix.reshape(1, D), w_main, b_main, w_gates, b_gates, S)

    c_mq = aq_w
    c_mv = aq_w + 2 * mqk_w
    c_mo = c_mv + mv_w
    c_ga = c_mo + mv_w
    c_gm = c_ga + D
    c_ak = c_gm + D
    c_av = c_ak + akv_w

    half = ROPE_DIM // 2
    inv_freq = ROPE_THETA ** (-jnp.arange(half, dtype=F32) / half)
    lane_d = jnp.arange(LANES) % ATTN_HEAD_DIM
    inv_lane = jnp.where(lane_d < ROPE_DIM, inv_freq[lane_d % half], 0.0).reshape(1, LANES).astype(F32)
    pos_f = positions.astype(F32).reshape(N, 1)
    attn_o = _swa_attention(proj, pos_f, inv_lane, sinks, B, S, aq_w, c_ak // LANES, c_av // LANES)

    mls_o = _mlstm(proj, gates, conv_qk, g_mlstm.reshape(1, mv_w), B, S,
                   c_mq // (2 * mqk_w), c_mv // mv_w, c_mo // mv_w, 2 * mqk_w, mv_w)

    E = N_EXPERTS
    wr = jnp.concatenate([w_router, jnp.zeros((D, LANES - E), F32)], axis=1)
    br = jnp.concatenate([b_router, jnp.full((LANES - E,), -1e30, F32)]).reshape(1, LANES)
    x1, h2, logits = _mix_out(attn_o, mls_o, proj, x2, mod3, g_post_mix.reshape(1, D), g_pre_ffn.reshape(1, D),
                              w_attn_proj.astype(BF16), w_mlstm_proj.astype(BF16), w_out.astype(BF16), wr, br,
                              S, c_ga // D, c_gm // D)

    top_e, top_w, rank, counts = _route(logits)
    cnt = counts[:, 0].astype(I32)
    padded = (cnt + ffn_tm - 1) // ffn_tm * ffn_tm
    pad_ends = jnp.cumsum(padded)
    pad_starts = pad_ends - padded
    n_rows = N * TOP_K + E * ffn_tm
    n_tiles = n_rows // ffn_tm
    tile_start = jnp.arange(n_tiles, dtype=I32) * ffn_tm
    tile_e = jnp.minimum(jnp.sum((pad_ends[None, :] <= tile_start[:, None]).astype(I32), axis=1), E - 1)
    n_valid = (pad_ends[E - 1:E] // ffn_tm).astype(I32)
    dest = _dest_rows(top_e, rank, jnp.broadcast_to(pad_starts.astype(F32)[:, None], (E, LANES)))

    xs = _dispatch(dest, h2, n_rows)
    act = _ffn_up(tile_e, n_valid, xs, w_gate_up, b_gate_up, ffn_tm)
    y_rows = _ffn_down(tile_e, n_valid, act, w_down, b_down, ffn_tm)
    out = _combine(dest, top_w.T, x1, mod3, g_post_ffn.reshape(1, D), y_rows, S)
    return out.reshape(B, S, D)


def kernel(x, c, positions, w_ada, b_ada, g_pre_mix, g_post_mix, g_pre_ffn, g_post_ffn, w_in, b_in, conv_qk, sinks,
           g_mlstm, w_attn_proj, w_mlstm_proj, w_out, w_router, b_router, w_gate_up, b_gate_up, w_down, b_down):
    B, S, D = x.shape
    depth = w_ada.shape[0]
    for l in range(depth):
        mod3 = _ada_ln(c, w_ada[l], b_ada[l]).reshape(B, 6, D)
        x = _block(x, mod3, positions, g_pre_mix[l], g_post_mix[l], g_pre_ffn[l], g_post_ffn[l], w_in[l], b_in[l],
                   conv_qk[l], sinks[l], g_mlstm[l], w_attn_proj[l], w_mlstm_proj[l], w_out[l], w_router[l],
                   b_router[l], w_gate_up[l], b_gate_up[l], w_down[l], b_down[l])
    return x
```

```python
import functools

import jax
import jax.numpy as jnp
from jax import lax
from jax.experimental import pallas as pl
from jax.experimental.pallas import tpu as pltpu

F32 = jnp.float32
BF16 = jnp.bfloat16
I32 = jnp.int32
HIGHEST = lax.Precision.HIGHEST

ATTN_HEAD_DIM = 64
ATTN_GROUP = 8
WINDOW = 128
ATTN_BLOCK = 128
ROPE_THETA = 500000.0
ROPE_DIM = ATTN_HEAD_DIM // 4
MLSTM_HEADS = 4
MLSTM_CHUNK = 128
CONV_WIDTH = 4
GATE_SOFTCAP = 15.0
N_EXPERTS = 32
TOP_K = 4
SWIGLU_ALPHA = 1.702
SWIGLU_LIMIT = 7.0
NORM_EPS = 1e-6

LANES = 128
SUBLANES = 8
VMEM_LIMIT = 56 * 1024 * 1024

NEG_INF = float("-inf")


def _cparams(*sem):
    return pltpu.CompilerParams(dimension_semantics=sem, vmem_limit_bytes=VMEM_LIMIT)


def _rms(x, g):
    ms = jnp.mean(x * x, axis=-1, keepdims=True)
    return x * lax.rsqrt(ms + NORM_EPS) * g


def _sigmoid(x):
    return 1.0 / (1.0 + jnp.exp(-x))


def _ada_body(c_ref, w_ref, b_ref, o_ref):
    c = c_ref[...]
    cond = c * _sigmoid(c)
    o_ref[...] = jnp.dot(cond, w_ref[...], preferred_element_type=F32, precision=HIGHEST) + b_ref[...]


def _ada_ln(c, w_ada, b_ada):
    B, D = c.shape
    W = w_ada.shape[1]
    tn = 1024
    cp = jnp.zeros((SUBLANES, D), F32).at[:B].set(c)
    out = pl.pallas_call(
        _ada_body,
        grid=(W // tn,),
        in_specs=[pl.BlockSpec((SUBLANES, D), lambda j: (0, 0)),
                  pl.BlockSpec((D, tn), lambda j: (0, j)),
                  pl.BlockSpec((1, tn), lambda j: (0, j))],
        out_specs=pl.BlockSpec((SUBLANES, tn), lambda j: (0, j)),
        out_shape=jax.ShapeDtypeStruct((SUBLANES, W), F32),
        compiler_params=_cparams("arbitrary"),
        name="ada_ln",
    )(cp, w_ada, b_ada.reshape(1, W))
    return out[:B]


def _inproj_body(x_ref, mod_ref, g_ref, w_ref, b_ref, wg_ref, bg_ref, o_ref, og_ref, h_ref):
    @pl.when(pl.program_id(1) == 0)
    def _():
        h = _rms(x_ref[...], g_ref[...]) * (1.0 + mod_ref[0, 1:2, :]) + mod_ref[0, 0:1, :]
        hb = h.astype(BF16)
        h_ref[...] = hb
        og_ref[...] = jnp.dot(hb, wg_ref[...], preferred_element_type=F32) + bg_ref[...]

    acc = jnp.dot(h_ref[...], w_ref[...], preferred_element_type=F32) + b_ref[...]
    o_ref[...] = acc.astype(BF16)


def _in_proj(x2, mod3, g, w_main, b_main, w_side, b_side, rows_per_batch, tm=1024, tn=1024):
    N, D = x2.shape
    W = w_main.shape[1]
    WS = w_side.shape[1]
    tpb = rows_per_batch // tm
    return pl.pallas_call(
        _inproj_body,
        grid=(N // tm, W // tn),
        in_specs=[pl.BlockSpec((tm, D), lambda i, j: (i, 0)),
                  pl.BlockSpec((1, 6, D), lambda i, j: (i // tpb, 0, 0)),
                  pl.BlockSpec((1, D), lambda i, j: (0, 0)),
                  pl.BlockSpec((D, tn), lambda i, j: (0, j)),
                  pl.BlockSpec((1, tn), lambda i, j: (0, j)),
                  pl.BlockSpec((D, WS), lambda i, j: (0, 0)),
                  pl.BlockSpec((1, WS), lambda i, j: (0, 0))],
        out_specs=[pl.BlockSpec((tm, tn), lambda i, j: (i, j)),
                   pl.BlockSpec((tm, WS), lambda i, j: (i, 0))],
        out_shape=[jax.ShapeDtypeStruct((N, W), BF16),
                   jax.ShapeDtypeStruct((N, WS), F32)],
        scratch_shapes=[pltpu.VMEM((tm, D), BF16)],
        compiler_params=_cparams("arbitrary", "arbitrary"),
        name="in_proj",
    )(x2, mod3, g, w_main, b_main, w_side, b_side)


def _attn_body(sinks_ref, q_ref, kc_ref, kp_ref, vc_ref, vp_ref, posc_ref, posp_ref, inv_ref, o_ref):
    n = pl.program_id(1)
    L = ATTN_BLOCK
    inv = inv_ref[...]
    lane = lax.broadcasted_iota(I32, (1, LANES), 1)
    first_half = (lane % ATTN_HEAD_DIM) < (ROPE_DIM // 2)
    low_lanes = lane < ATTN_HEAD_DIM

    ang_c = posc_ref[...] * inv
    ang_p = posp_ref[...] * inv
    cos_c, sin_c = jnp.cos(ang_c), jnp.sin(ang_c)
    cos_p, sin_p = jnp.cos(ang_p), jnp.sin(ang_p)

    def rope(x, cs, sn):
        swapped = jnp.where(first_half, -pltpu.roll(x, LANES - ROPE_DIM // 2, 1), pltpu.roll(x, ROPE_DIM // 2, 1))
        return x * cs + swapped * sn

    k2 = jnp.concatenate([rope(kp_ref[...].astype(F32), cos_p, sin_p),
                          rope(kc_ref[...].astype(F32), cos_c, sin_c)], axis=0)
    v2 = jnp.concatenate([vp_ref[...], vc_ref[...]], axis=0).astype(F32)
    k2s = pltpu.roll(k2, ATTN_HEAD_DIM, 1)
    v2s = pltpu.roll(v2, ATTN_HEAD_DIM, 1)
    zero = jnp.zeros_like(k2)
    k_lo = [jnp.where(low_lanes, k2, zero).astype(BF16), jnp.where(low_lanes, k2s, zero).astype(BF16)]
    k_hi = [jnp.where(low_lanes, zero, k2s).astype(BF16), jnp.where(low_lanes, zero, k2).astype(BF16)]
    v_lo = [jnp.where(low_lanes, v2, zero).astype(BF16), jnp.where(low_lanes, v2s, zero).astype(BF16)]
    v_hi = [jnp.where(low_lanes, zero, v2s).astype(BF16), jnp.where(low_lanes, zero, v2).astype(BF16)]

    qi = lax.broadcasted_iota(I32, (L, 2 * L), 0)
    kj = lax.broadcasted_iota(I32, (L, 2 * L), 1)
    diff = qi + L - kj
    valid = (diff >= 0) & (diff < WINDOW) & ((kj >= L) | (n > 0))

    scale = ATTN_HEAD_DIM ** -0.5
    n_pairs = q_ref.shape[1] // LANES
    for j in range(n_pairs):
        hk = (2 * j) // ATTN_GROUP
        qp = (rope(q_ref[:, j * LANES:(j + 1) * LANES].astype(F32), cos_c, sin_c) * scale).astype(BF16)
        acc = jnp.zeros((L, LANES), F32)
        for half, (kx, vx) in enumerate(((k_lo[hk], v_lo[hk]), (k_hi[hk], v_hi[hk]))):
            s = lax.dot_general(qp, kx, (((1,), (1,)), ((), ())), preferred_element_type=F32)
            s = jnp.where(valid, s, NEG_INF)
            sink = sinks_ref[2 * j + half]
            m = jnp.maximum(jnp.max(s, axis=-1, keepdims=True), sink)
            p = jnp.exp(s - m)
            denom = jnp.sum(p, axis=-1, keepdims=True) + jnp.exp(sink - m)
            acc = acc + jnp.dot(p.astype(BF16), vx, preferred_element_type=F32) / denom
        o_ref[:, j * LANES:(j + 1) * LANES] = acc.astype(BF16)


def _swa_attention(proj, side, pos_f, inv_lane, sinks, B, S, q_width, k_blk, v_blk):
    L = ATTN_BLOCK
    nb = S // L

    def cur(b, n, s):
        return b * nb + n

    def prev(b, n, s):
        return b * nb + jnp.maximum(n - 1, 0)

    grid_spec = pltpu.PrefetchScalarGridSpec(
        num_scalar_prefetch=1,
        grid=(B, nb),
        in_specs=[pl.BlockSpec((L, q_width), lambda b, n, s: (cur(b, n, s), 0)),
                  pl.BlockSpec((L, LANES), lambda b, n, s: (cur(b, n, s), k_blk)),
                  pl.BlockSpec((L, LANES), lambda b, n, s: (prev(b, n, s), k_blk)),
                  pl.BlockSpec((L, LANES), lambda b, n, s: (cur(b, n, s), v_blk)),
                  pl.BlockSpec((L, LANES), lambda b, n, s: (prev(b, n, s), v_blk)),
                  pl.BlockSpec((L, 1), lambda b, n, s: (cur(b, n, s), 0)),
                  pl.BlockSpec((L, 1), lambda b, n, s: (prev(b, n, s), 0)),
                  pl.BlockSpec((1, LANES), lambda b, n, s: (0, 0))],
        out_specs=pl.BlockSpec((L, q_width), lambda b, n, s: (cur(b, n, s), 0)),
    )
    return pl.pallas_call(
        _attn_body,
        grid_spec=grid_spec,
        out_shape=jax.ShapeDtypeStruct((B * S, q_width), BF16),
        compiler_params=_cparams("arbitrary", "arbitrary"),
        name="swa_attn",
    )(sinks, proj, side, side, side, side, pos_f, pos_f, inv_lane)


def _log_sigmoid(z):
    return jnp.minimum(z, 0.0) - jnp.log1p(jnp.exp(-jnp.abs(z)))


def _mlstm_body(qk_ref, v_ref, og_ref, gates_ref, cw_ref, gn_ref, out_ref, hist_ref, c_ref, m_ref):
    L = MLSTM_CHUNK
    H = MLSTM_HEADS
    dqk = qk_ref.shape[1] // (2 * H)
    dv = v_ref.shape[1] // H
    tail = SUBLANES

    @pl.when(pl.program_id(1) == 0)
    def _():
        hist_ref[0:tail, :] = jnp.zeros((tail, hist_ref.shape[1]), F32)
        c_ref[...] = jnp.zeros(c_ref.shape, F32)
        m_ref[...] = jnp.zeros(m_ref.shape, F32)

    hist_ref[tail:tail + L, :] = qk_ref[...].astype(F32)
    conv = hist_ref[tail:tail + L, :] * cw_ref[CONV_WIDTH - 1:CONV_WIDTH, :]
    for t in range(CONV_WIDTH - 1):
        off = tail - (CONV_WIDTH - 1) + t
        conv = conv + hist_ref[off:off + L, :] * cw_ref[t:t + 1, :]
    hist_ref[0:tail, :] = hist_ref[L:L + tail, :]
    qk = conv * _sigmoid(conv)

    gates = gates_ref[...]
    capped = GATE_SOFTCAP * jnp.tanh(gates / GATE_SOFTCAP)
    log_f = _log_sigmoid(capped)
    row = lax.broadcasted_iota(I32, (L, L), 0)
    col = lax.broadcasted_iota(I32, (L, L), 1)
    causal = col <= row
    b_all = jnp.dot(causal.astype(F32), log_f, preferred_element_type=F32, precision=HIGHEST)
    b_all_t = b_all.T
    capped_t = capped.T
    ones_col = jnp.where(lax.broadcasted_iota(I32, (L, LANES), 1) == 0, 1.0, 0.0).astype(BF16)

    for h in range(H):
        b_col = b_all[:, H + h:H + h + 1]
        li_col = capped[:, h:h + 1]
        b_row = b_all_t[H + h:H + h + 1, :]
        li_row = capped_t[h:h + 1, :]
        g_last = b_all[L - 1:L, H + h:H + h + 1]
        m_prev = m_ref[h:h + 1, 0:1]

        d_log = jnp.where(causal, b_col - b_row + li_row, NEG_INF)
        inter = b_col + m_prev
        m = jnp.maximum(inter, jnp.max(d_log, axis=-1, keepdims=True))
        q_h = qk[:, h * dqk:(h + 1) * dqk].astype(BF16)
        k_f = qk[:, (H + h) * dqk:(H + h + 1) * dqk] * (dqk ** -0.5)
        s_qk = lax.dot_general(q_h, k_f.astype(BF16), (((1,), (1,)), ((), ())),
                               preferred_element_type=F32) * jnp.exp(d_log - m)
        v_ext = jnp.concatenate([v_ref[:, h * dv:(h + 1) * dv], ones_col], axis=1)
        c_prev = c_ref[h]
        tot = (jnp.dot(s_qk.astype(BF16), v_ext, preferred_element_type=F32)
               + jnp.exp(inter - m) * jnp.dot(q_h, c_prev.astype(BF16), preferred_element_type=F32))
        h_t = tot[:, :dv] / jnp.maximum(jnp.abs(tot[:, dv:dv + 1]), jnp.exp(-m))

        a_col = g_last - b_col + li_col
        m_new = jnp.maximum(g_last + m_prev, jnp.max(a_col, axis=0, keepdims=True))
        wk = jnp.exp(a_col - m_new) * k_f
        kv = jnp.dot(wk.T.astype(BF16), v_ext, preferred_element_type=F32)
        c_ref[h] = jnp.exp(g_last + m_prev - m_new) * c_prev + kv
        m_ref[h:h + 1, :] = jnp.broadcast_to(m_new, (1, LANES))

        hn = _rms(h_t, gn_ref[:, h * dv:(h + 1) * dv])
        out_ref[:, h * dv:(h + 1) * dv] = (_sigmoid(og_ref[:, h * dv:(h + 1) * dv].astype(F32)) * hn).astype(BF16)


def _mlstm(proj, side, conv_w, g_norm, B, S, qk_blk, v_blk, og_blk, gates_blk, qk_width, v_width):
    L = MLSTM_CHUNK
    nc = S // L
    dv = v_width // MLSTM_HEADS
    dqk = qk_width // (2 * MLSTM_HEADS)

    def rows(b, c):
        return b * nc + c

    return pl.pallas_call(
        _mlstm_body,
        grid=(B, nc),
        in_specs=[pl.BlockSpec((L, qk_width), lambda b, c: (rows(b, c), qk_blk)),
                  pl.BlockSpec((L, v_width), lambda b, c: (rows(b, c), v_blk)),
                  pl.BlockSpec((L, v_width), lambda b, c: (rows(b, c), og_blk)),
                  pl.BlockSpec((L, LANES), lambda b, c: (rows(b, c), gates_blk)),
                  pl.BlockSpec((CONV_WIDTH, qk_width), lambda b, c: (0, 0)),
                  pl.BlockSpec((1, v_width), lambda b, c: (0, 0))],
        out_specs=pl.BlockSpec((L, v_width), lambda b, c: (rows(b, c), 0)),
        out_shape=jax.ShapeDtypeStruct((B * S, v_width), BF16),
        scratch_shapes=[pltpu.VMEM((L + 2 * SUBLANES, qk_width), F32),
                        pltpu.VMEM((MLSTM_HEADS, dqk, dv + LANES), F32),
                        pltpu.VMEM((SUBLANES, LANES), F32)],
        compiler_params=_cparams("arbitrary", "arbitrary"),
        name="mlstm",
    )(proj, proj, proj, side, conv_w, g_norm)


def _merge_body(a_ref, m_ref, ga_ref, gm_ref, wa_ref, wm_ref, o_ref):
    ya = jnp.dot(a_ref[...], wa_ref[...], preferred_element_type=F32)
    ym = jnp.dot(m_ref[...], wm_ref[...], preferred_element_type=F32)
    merged = _sigmoid(ga_ref[...].astype(F32)) * ya + _sigmoid(gm_ref[...].astype(F32)) * ym
    o_ref[...] = merged.astype(BF16)


def _merge(attn_o, mls_o, proj, wa, wm, ga_blk, gm_blk, tm=1024):
    N = attn_o.shape[0]
    D = wa.shape[1]
    wa_w, wm_w = attn_o.shape[1], mls_o.shape[1]
    const = lambda i: (0, 0)
    return pl.pallas_call(
        _merge_body,
        grid=(N // tm,),
        in_specs=[pl.BlockSpec((tm, wa_w), lambda i: (i, 0)),
                  pl.BlockSpec((tm, wm_w), lambda i: (i, 0)),
                  pl.BlockSpec((tm, D), lambda i: (i, ga_blk)),
                  pl.BlockSpec((tm, D), lambda i: (i, gm_blk)),
                  pl.BlockSpec((wa_w, D), const),
                  pl.BlockSpec((wm_w, D), const)],
        out_specs=pl.BlockSpec((tm, D), lambda i: (i, 0)),
        out_shape=jax.ShapeDtypeStruct((N, D), BF16),
        compiler_params=_cparams("arbitrary"),
        name="merge",
    )(attn_o, mls_o, proj, proj, wa, wm)


def _split_bf16(v):
    hi = v.astype(BF16)
    return hi, (v - hi.astype(F32)).astype(BF16)


def _outproj_body(mg_ref, x_ref, mod_ref, gpost_ref, gpre_ref, wo_ref, wr_ref, br_ref, x1_ref, h2_ref, lg_ref):
    y = jnp.dot(mg_ref[...], wo_ref[...], preferred_element_type=F32)
    x1 = x_ref[...] + mod_ref[0, 2:3, :] * _rms(y, gpost_ref[...])
    x1_ref[...] = x1
    h2 = _rms(x1, gpre_ref[...]) * (1.0 + mod_ref[0, 4:5, :]) + mod_ref[0, 3:4, :]
    h2_ref[...] = h2
    h_hi, h_lo = _split_bf16(h2)
    w_hi, w_lo = _split_bf16(wr_ref[...])
    lg = (jnp.dot(h_hi, w_hi, preferred_element_type=F32) + jnp.dot(h_hi, w_lo, preferred_element_type=F32)
          + jnp.dot(h_lo, w_hi, preferred_element_type=F32))
    lg_ref[...] = lg + br_ref[...]


def _out_proj(merged, x2, mod3, g_post, g_pre, wo, wr, br, rows_per_batch, tm=512):
    N, D = x2.shape
    tpb = rows_per_batch // tm
    const = lambda i: (0, 0)
    return pl.pallas_call(
        _outproj_body,
        grid=(N // tm,),
        in_specs=[pl.BlockSpec((tm, D), lambda i: (i, 0)),
                  pl.BlockSpec((tm, D), lambda i: (i, 0)),
                  pl.BlockSpec((1, 6, D), lambda i: (i // tpb, 0, 0)),
                  pl.BlockSpec((1, D), const),
                  pl.BlockSpec((1, D), const),
                  pl.BlockSpec((D, D), const),
                  pl.BlockSpec((D, LANES), const),
                  pl.BlockSpec((1, LANES), const)],
        out_specs=[pl.BlockSpec((tm, D), lambda i: (i, 0)),
                   pl.BlockSpec((tm, D), lambda i: (i, 0)),
                   pl.BlockSpec((tm, LANES), lambda i: (i, 0))],
        out_shape=[jax.ShapeDtypeStruct((N, D), F32),
                   jax.ShapeDtypeStruct((N, D), F32),
                   jax.ShapeDtypeStruct((N, LANES), F32)],
        compiler_params=_cparams("arbitrary"),
        name="out_proj",
    )(merged, x2, mod3, g_post, g_pre, wo, wr, br)


def _route_body(lg_ref, e_ref, w_ref, r_ref, cnt_ref, run_ref):
    T = lg_ref.shape[0]
    E = N_EXPERTS

    @pl.when(pl.program_id(0) == 0)
    def _():
        run_ref[...] = jnp.zeros(run_ref.shape, F32)

    lg = lg_ref[...].T[0:E, :]
    eidx = lax.broadcasted_iota(I32, (E, T), 0)
    tops, idxs, hots = [], [], []
    for _ in range(TOP_K):
        mx = jnp.max(lg, axis=0, keepdims=True)
        ik = jnp.min(jnp.where(lg == mx, eidx, E), axis=0, keepdims=True)
        hot = eidx == ik
        lg = jnp.where(hot, NEG_INF, lg)
        tops.append(mx)
        idxs.append(ik)
        hots.append(hot)
    exps = [jnp.exp(t - tops[0]) for t in tops]
    tot = exps[0] + exps[1] + exps[2] + exps[3]

    sel = jnp.zeros((E, T), F32)
    for hot in hots:
        sel = sel + jnp.where(hot, 1.0, 0.0)
    before = lax.broadcasted_iota(I32, (T, T), 0) < lax.broadcasted_iota(I32, (T, T), 1)
    excl = jnp.dot(sel.astype(BF16), jnp.where(before, 1.0, 0.0).astype(BF16), preferred_element_type=F32)
    base = excl + run_ref[:, 0:1]
    for k in range(TOP_K):
        e_ref[k:k + 1, :] = idxs[k]
        w_ref[k:k + 1, :] = exps[k] / tot
        r_ref[k:k + 1, :] = jnp.sum(jnp.where(hots[k], base, 0.0), axis=0, keepdims=True).astype(I32)
    run_ref[...] = run_ref[...] + jnp.sum(sel, axis=1, keepdims=True)
    cnt_ref[...] = run_ref[...]


def _route(logits, T=512):
    N = logits.shape[0]
    return pl.pallas_call(
        _route_body,
        grid=(N // T,),
        in_specs=[pl.BlockSpec((T, LANES), lambda i: (i, 0))],
        out_specs=[pl.BlockSpec((TOP_K, T), lambda i: (0, i)),
                   pl.BlockSpec((TOP_K, T), lambda i: (0, i)),
                   pl.BlockSpec((TOP_K, T), lambda i: (0, i)),
                   pl.BlockSpec((N_EXPERTS, LANES), lambda i: (0, 0))],
        out_shape=[jax.ShapeDtypeStruct((TOP_K, N), I32),
                   jax.ShapeDtypeStruct((TOP_K, N), F32),
                   jax.ShapeDtypeStruct((TOP_K, N), I32),
                   jax.ShapeDtypeStruct((N_EXPERTS, LANES), F32)],
        scratch_shapes=[pltpu.VMEM((N_EXPERTS, LANES), F32)],
        compiler_params=_cparams("arbitrary"),
        name="route",
    )(logits)


def _dest_body(e_ref, r_ref, ps_ref, d_ref):
    T = e_ref.shape[1]
    eidx = lax.broadcasted_iota(I32, (N_EXPERTS, T), 0)
    starts = ps_ref[:, 0:1]
    for k in range(TOP_K):
        base = jnp.sum(jnp.where(eidx == e_ref[k:k + 1, :], starts, 0.0), axis=0, keepdims=True)
        d_ref[k:k + 1, :] = r_ref[k:k + 1, :] + base.astype(I32)


def _dest_rows(top_e, rank, pad_starts_f, T=2048):
    N = top_e.shape[1]
    return pl.pallas_call(
        _dest_body,
        grid=(N // T,),
        in_specs=[pl.BlockSpec((TOP_K, T), lambda i: (0, i)),
                  pl.BlockSpec((TOP_K, T), lambda i: (0, i)),
                  pl.BlockSpec((N_EXPERTS, LANES), lambda i: (0, 0))],
        out_specs=pl.BlockSpec((TOP_K, T), lambda i: (0, i)),
        out_shape=jax.ShapeDtypeStruct((TOP_K, N), I32),
        compiler_params=_cparams("arbitrary"),
        name="dest_rows",
    )(top_e, rank, pad_starts_f)


def _dispatch_body(dest_ref, h_ref, xs_in, xs_hbm, sem):
    del xs_in
    tc = h_ref.shape[0]

    def row_copy(t, k):
        return pltpu.make_async_copy(h_ref.at[pl.ds(t, 1), :], xs_hbm.at[pl.ds(dest_ref[k, t], 1), :], sem)

    def issue(t, carry):
        for k in range(TOP_K):
            row_copy(t, k).start()
        return carry

    def drain(t, carry):
        for k in range(TOP_K):
            row_copy(t, k).wait()
        return carry

    lax.fori_loop(0, tc, issue, 0)
    lax.fori_loop(0, tc, drain, 0)


def _dispatch(dest, h2, n_rows, tc=256):
    N, D = h2.shape
    xs0 = jnp.zeros((n_rows, D), h2.dtype)
    return pl.pallas_call(
        _dispatch_body,
        grid=(N // tc,),
        in_specs=[pl.BlockSpec((TOP_K, tc), lambda i: (0, i), memory_space=pltpu.SMEM),
                  pl.BlockSpec((tc, D), lambda i: (i, 0)),
                  pl.BlockSpec(memory_space=pl.ANY)],
        out_specs=pl.BlockSpec(memory_space=pl.ANY),
        out_shape=jax.ShapeDtypeStruct((n_rows, D), h2.dtype),
        scratch_shapes=[pltpu.SemaphoreType.DMA(())],
        input_output_aliases={2: 0},
        compiler_params=_cparams("arbitrary"),
        name="dispatch",
    )(dest, h2, xs0)


PAIR_CHUNK = 2 * LANES
UP_CHUNK = 1024


def _new_weights(te_ref, i):
    return (i == 0) | (te_ref[i] != te_ref[jnp.maximum(i - 1, 0)])


def _ffn_up_body(te_ref, nv_ref, x_ref, w_ref, b_ref, sel_ref, o_ref, wb_ref):
    i = pl.program_id(1)

    @pl.when((i < nv_ref[0]) & _new_weights(te_ref, i))
    def _():
        wb_ref[...] = w_ref[0].astype(BF16)

    @pl.when(i < nv_ref[0])
    def _():
        xb = x_ref[...].astype(BF16)
        for c in range(wb_ref.shape[1] // UP_CHUNK):
            cols = slice(c * UP_CHUNK, (c + 1) * UP_CHUNK)
            gu = jnp.dot(xb, wb_ref[:, cols], preferred_element_type=F32) + b_ref[0, :, cols]
            glu = jnp.minimum(gu, SWIGLU_LIMIT)
            half = 0.5 * glu
            gate = half + half * jnp.tanh(SWIGLU_ALPHA * half)
            lin = jnp.clip(gu, -SWIGLU_LIMIT, SWIGLU_LIMIT) + 1.0
            for p in range(UP_CHUNK // PAIR_CHUNK):
                parts = []
                for v in range(PAIR_CHUNK // LANES):
                    lo = p * PAIR_CHUNK + v * LANES
                    parts.append(gate[:, lo:lo + LANES] * pltpu.roll(lin[:, lo:lo + LANES], LANES - 1, 1))
                prod = jnp.concatenate(parts, axis=1).astype(BF16)
                out_lo = (c * UP_CHUNK + p * PAIR_CHUNK) // 2
                o_ref[:, out_lo:out_lo + LANES] = jnp.dot(prod, sel_ref[...], preferred_element_type=F32).astype(BF16)

    @pl.when(i >= nv_ref[0])
    def _():
        o_ref[...] = jnp.zeros(o_ref.shape, o_ref.dtype)


def _ffn_up(tile_e, n_valid, xs, w_gate_up, b_gate_up, tm, tf=512):
    P, D = xs.shape
    E, _, F2 = w_gate_up.shape
    F = F2 // 2
    sel = (jnp.arange(PAIR_CHUNK)[:, None] == 2 * jnp.arange(LANES)[None, :]).astype(BF16)

    def row(i, nv):
        return jnp.minimum(i, nv[0] - 1)

    grid_spec = pltpu.PrefetchScalarGridSpec(
        num_scalar_prefetch=2,
        grid=(F // tf, P // tm),
        in_specs=[pl.BlockSpec((tm, D), lambda j, i, te, nv: (row(i, nv), 0)),
                  pl.BlockSpec((1, D, 2 * tf), lambda j, i, te, nv: (te[row(i, nv)], 0, j)),
                  pl.BlockSpec((1, 1, 2 * tf), lambda j, i, te, nv: (te[row(i, nv)], 0, j)),
                  pl.BlockSpec((PAIR_CHUNK, LANES), lambda j, i, te, nv: (0, 0))],
        out_specs=pl.BlockSpec((tm, tf), lambda j, i, te, nv: (i, j)),
        scratch_shapes=[pltpu.VMEM((D, 2 * tf), BF16)],
    )
    return pl.pallas_call(
        _ffn_up_body,
        grid_spec=grid_spec,
        out_shape=jax.ShapeDtypeStruct((P, F), BF16),
        compiler_params=_cparams("arbitrary", "arbitrary"),
        name="ffn_up",
    )(tile_e, n_valid, xs, w_gate_up, b_gate_up.reshape(E, 1, F2), sel)


def _ffn_down_body(te_ref, nv_ref, a_ref, w_ref, b_ref, o_ref, wb_ref):
    i = pl.program_id(1)

    @pl.when((i < nv_ref[0]) & _new_weights(te_ref, i))
    def _():
        wb_ref[...] = w_ref[0].astype(BF16)

    @pl.when(i < nv_ref[0])
    def _():
        o_ref[...] = jnp.dot(a_ref[...], wb_ref[...], preferred_element_type=F32) + b_ref[0]

    @pl.when(i >= nv_ref[0])
    def _():
        o_ref[...] = jnp.zeros(o_ref.shape, o_ref.dtype)


def _ffn_down(tile_e, n_valid, act, w_down, b_down, tm, tn=1024):
    P, F = act.shape
    E, _, D = w_down.shape

    def row(i, nv):
        return jnp.minimum(i, nv[0] - 1)

    grid_spec = pltpu.PrefetchScalarGridSpec(
        num_scalar_prefetch=2,
        grid=(D // tn, P // tm),
        in_specs=[pl.BlockSpec((tm, F), lambda j, i, te, nv: (row(i, nv), 0)),
                  pl.BlockSpec((1, F, tn), lambda j, i, te, nv: (te[row(i, nv)], 0, j)),
                  pl.BlockSpec((1, 1, tn), lambda j, i, te, nv: (te[row(i, nv)], 0, j))],
        out_specs=pl.BlockSpec((tm, tn), lambda j, i, te, nv: (i, j)),
        scratch_shapes=[pltpu.VMEM((F, tn), BF16)],
    )
    return pl.pallas_call(
        _ffn_down_body,
        grid_spec=grid_spec,
        out_shape=jax.ShapeDtypeStruct((P, D), F32),
        compiler_params=_cparams("arbitrary", "arbitrary"),
        name="ffn_down",
    )(tile_e, n_valid, act, w_down, b_down.reshape(E, 1, D))


def _combine_body(dcur_ref, dnxt_ref, w_ref, x1_ref, mod_ref, g_ref, y_hbm, o_ref, buf_ref, sem_ref):
    i = pl.program_id(0)
    n = pl.num_programs(0)
    tt = x1_ref.shape[0]
    slot = i % 2

    def row_copy(d_ref, s, t, k):
        return pltpu.make_async_copy(y_hbm.at[pl.ds(d_ref[k, t], 1), :],
                                     buf_ref.at[s, k, pl.ds(t, 1), :], sem_ref.at[s])

    def issue(d_ref, s):
        def body(t, carry):
            for k in range(TOP_K):
                row_copy(d_ref, s, t, k).start()
            return carry
        lax.fori_loop(0, tt, body, 0)

    @pl.when(i == 0)
    def _():
        issue(dcur_ref, 0)

    @pl.when(i + 1 < n)
    def _():
        issue(dnxt_ref, 1 - slot)

    def drain(t, carry):
        for k in range(TOP_K):
            row_copy(dcur_ref, slot, t, k).wait()
        return carry
    lax.fori_loop(0, tt, drain, 0)

    acc = jnp.zeros(x1_ref.shape, F32)
    for k in range(TOP_K):
        acc = acc + w_ref[:, k:k + 1] * buf_ref[slot, k]
    o_ref[...] = x1_ref[...] + mod_ref[0, 5:6, :] * _rms(acc, g_ref[...])


def _combine(dest, top_w_t, x1, mod3, g_post, y_rows, rows_per_batch, tt=128):
    N, D = x1.shape
    nt = N // tt
    tpb = rows_per_batch // tt
    return pl.pallas_call(
        _combine_body,
        grid=(nt,),
        in_specs=[pl.BlockSpec((TOP_K, tt), lambda i: (0, i), memory_space=pltpu.SMEM),
                  pl.BlockSpec((TOP_K, tt), lambda i: (0, jnp.minimum(i + 1, nt - 1)), memory_space=pltpu.SMEM),
                  pl.BlockSpec((tt, TOP_K), lambda i: (i, 0)),
                  pl.BlockSpec((tt, D), lambda i: (i, 0)),
                  pl.BlockSpec((1, 6, D), lambda i: (i // tpb, 0, 0)),
                  pl.BlockSpec((1, D), lambda i: (0, 0)),
                  pl.BlockSpec(memory_space=pl.ANY)],
        out_specs=pl.BlockSpec((tt, D), lambda i: (i, 0)),
        out_shape=jax.ShapeDtypeStruct((N, D), F32),
        scratch_shapes=[pltpu.VMEM((2, TOP_K, tt, D), F32),
                        pltpu.SemaphoreType.DMA((2,))],
        compiler_params=_cparams("arbitrary"),
        name="combine",
    )(dest, dest, top_w_t, x1, mod3, g_post, y_rows)


def _block(x, mod3, positions, g_pre_mix, g_post_mix, g_pre_ffn, g_post_ffn, w_in, b_in, conv_qk, sinks, g_mlstm,
           w_attn_proj, w_mlstm_proj, w_out, w_router, b_router, w_gate_up, b_gate_up, w_down, b_down, ffn_tm=512):
    B, S, D = x.shape
    N = B * S
    aq_w = w_attn_proj.shape[0]
    akv_w = (aq_w // ATTN_HEAD_DIM // ATTN_GROUP) * ATTN_HEAD_DIM
    mv_w = w_mlstm_proj.shape[0]
    mqk_w = mv_w // 2
    H = MLSTM_HEADS
    sizes = (aq_w, akv_w, akv_w, mqk_w, mqk_w, mv_w, mv_w, H, H, D, D)
    offs = [0]
    for sz in sizes:
        offs.append(offs[-1] + sz)
    seg_w = lambda k: w_in[:, offs[k]:offs[k + 1]]
    seg_b = lambda k: b_in[offs[k]:offs[k + 1]]
    order = (0, 3, 4, 5, 6, 9, 10)
    w_main = jnp.concatenate([seg_w(k) for k in order], axis=1).astype(BF16)
    b_main = jnp.concatenate([seg_b(k) for k in order]).reshape(1, -1)
    pad = LANES - 2 * H
    w_side = jnp.concatenate([seg_w(1), seg_w(2), seg_w(7), seg_w(8), jnp.zeros((D, pad), F32)], axis=1).astype(BF16)
    b_side = jnp.concatenate([seg_b(1), seg_b(2), seg_b(7), seg_b(8), jnp.zeros((pad,), F32)]).reshape(1, -1)

    x2 = x.reshape(N, D)
    proj, side = _in_proj(x2, mod3, g_pre_mix.reshape(1, D), w_main, b_main, w_side, b_side, S)

    c_mq = aq_w
    c_mv = aq_w + 2 * mqk_w
    c_mo = c_mv + mv_w
    c_ga = c_mo + mv_w
    c_gm = c_ga + D
    side_k, side_v, side_gates = 0, akv_w // LANES, 2 * akv_w // LANES

    half = ROPE_DIM // 2
    inv_freq = ROPE_THETA ** (-jnp.arange(half, dtype=F32) / half)
    lane_d = jnp.arange(LANES) % ATTN_HEAD_DIM
    inv_lane = jnp.where(lane_d < ROPE_DIM, inv_freq[lane_d % half], 0.0).reshape(1, LANES).astype(F32)
    pos_f = positions.astype(F32).reshape(N, 1)
    attn_o = _swa_attention(proj, side, pos_f, inv_lane, sinks, B, S, aq_w, side_k, side_v)

    mls_o = _mlstm(proj, side, conv_qk, g_mlstm.reshape(1, mv_w), B, S,
                   c_mq // (2 * mqk_w), c_mv // mv_w, c_mo // mv_w, side_gates, 2 * mqk_w, mv_w)

    E = N_EXPERTS
    wr = jnp.concatenate([w_router, jnp.zeros((D, LANES - E), F32)], axis=1)
    br = jnp.concatenate([b_router, jnp.full((LANES - E,), -1e30, F32)]).reshape(1, LANES)
    merged = _merge(attn_o, mls_o, proj, w_attn_proj.astype(BF16), w_mlstm_proj.astype(BF16), c_ga // D, c_gm // D)
    x1, h2, logits = _out_proj(merged, x2, mod3, g_post_mix.reshape(1, D), g_pre_ffn.reshape(1, D),
                               w_out.astype(BF16), wr, br, S)

    top_e, top_w, rank, counts = _route(logits)
    cnt = counts[:, 0].astype(I32)
    padded = (cnt + ffn_tm - 1) // ffn_tm * ffn_tm
    pad_ends = jnp.cumsum(padded)
    pad_starts = pad_ends - padded
    n_rows = N * TOP_K + E * ffn_tm
    n_tiles = n_rows // ffn_tm
    tile_start = jnp.arange(n_tiles, dtype=I32) * ffn_tm
    tile_e = jnp.minimum(jnp.sum((pad_ends[None, :] <= tile_start[:, None]).astype(I32), axis=1), E - 1)
    n_valid = (pad_ends[E - 1:E] // ffn_tm).astype(I32)
    dest = _dest_rows(top_e, rank, jnp.broadcast_to(pad_starts.astype(F32)[:, None], (E, LANES)))

    xs = _dispatch(dest, h2, n_rows)
    act = _ffn_up(tile_e, n_valid, xs, w_gate_up, b_gate_up, ffn_tm)
    y_rows = _ffn_down(tile_e, n_valid, act, w_down, b_down, ffn_tm)
    out = _combine(dest, top_w.T, x1, mod3, g_post_ffn.reshape(1, D), y_rows, S)
    return out.reshape(B, S, D)


def kernel(x, c, positions, w_ada, b_ada, g_pre_mix, g_post_mix, g_pre_ffn, g_post_ffn, w_in, b_in, conv_qk, sinks,
           g_mlstm, w_attn_proj, w_mlstm_proj, w_out, w_router, b_router, w_gate_up, b_gate_up, w_down, b_down):
    B, S, D = x.shape
    depth = w_ada.shape[0]
    for l in range(depth):
        mod3 = _ada_ln(c, w_ada[l], b_ada[l]).reshape(B, 6, D)
        x = _block(x, mod3, positions, g_pre_mix[l], g_post_mix[l], g_pre_ffn[l], g_post_ffn[l], w_in[l], b_in[l],
                   conv_qk[l], sinks[l], g_mlstm[l], w_attn_proj[l], w_mlstm_proj[l], w_out[l], w_router[l],
                   b_router[l], w_gate_up[l], b_gate_up[l], w_down[l], b_down[l])
    return x
```

```python
import jax
import jax.numpy as jnp
from jax import lax
from jax.experimental import pallas as pl
from jax.experimental.pallas import tpu as pltpu

F32 = jnp.float32
BF16 = jnp.bfloat16
I32 = jnp.int32
HIGHEST = lax.Precision.HIGHEST

ATTN_HEAD_DIM = 64
ATTN_GROUP = 8
WINDOW = 128
ATTN_BLOCK = 128
ROPE_THETA = 500000.0
ROPE_DIM = ATTN_HEAD_DIM // 4
MLSTM_HEADS = 4
MLSTM_CHUNK = 128
CONV_WIDTH = 4
GATE_SOFTCAP = 15.0
N_EXPERTS = 32
TOP_K = 4
SWIGLU_ALPHA = 1.702
SWIGLU_LIMIT = 7.0
NORM_EPS = 1e-6

LANES = 128
SUBLANES = 8
VMEM_LIMIT = 60 * 1024 * 1024

NEG_INF = float("-inf")


def _cparams(*sem):
    return pltpu.CompilerParams(dimension_semantics=sem, vmem_limit_bytes=VMEM_LIMIT)


def _rms(x, g):
    ms = jnp.mean(x * x, axis=-1, keepdims=True)
    return x * lax.rsqrt(ms + NORM_EPS) * g


def _sigmoid(x):
    return 1.0 / (1.0 + jnp.exp(-x))


def _ada_body(c_ref, w_ref, b_ref, o_ref):
    c = c_ref[...]
    cond = c * _sigmoid(c)
    o_ref[...] = jnp.dot(cond, w_ref[...], preferred_element_type=F32, precision=HIGHEST) + b_ref[...]


def _ada_ln(c, w_ada, b_ada):
    B, D = c.shape
    W = w_ada.shape[1]
    tn = 1024
    cp = jnp.zeros((SUBLANES, D), F32).at[:B].set(c)
    out = pl.pallas_call(
        _ada_body,
        grid=(W // tn,),
        in_specs=[pl.BlockSpec((SUBLANES, D), lambda j: (0, 0)),
                  pl.BlockSpec((D, tn), lambda j: (0, j)),
                  pl.BlockSpec((1, tn), lambda j: (0, j))],
        out_specs=pl.BlockSpec((SUBLANES, tn), lambda j: (0, j)),
        out_shape=jax.ShapeDtypeStruct((SUBLANES, W), F32),
        compiler_params=_cparams("arbitrary"),
        name="ada_ln",
    )(cp, w_ada, b_ada.reshape(1, W))
    return out[:B]


def _inproj_body(x_ref, mod_ref, g_ref, w_ref, b_ref, wg_ref, bg_ref, o_ref, og_ref, h_ref):
    @pl.when(pl.program_id(1) == 0)
    def _():
        h = _rms(x_ref[...], g_ref[...]) * (1.0 + mod_ref[0, 1:2, :]) + mod_ref[0, 0:1, :]
        hb = h.astype(BF16)
        h_ref[...] = hb
        og_ref[...] = jnp.dot(hb, wg_ref[...], preferred_element_type=F32) + bg_ref[...]

    acc = jnp.dot(h_ref[...], w_ref[...], preferred_element_type=F32) + b_ref[...]
    o_ref[...] = acc.astype(BF16)


def _in_proj(x2, mod3, g, w_main, b_main, w_side, b_side, rows_per_batch, tm=1024, tn=1024):
    N, D = x2.shape
    W = w_main.shape[1]
    WS = w_side.shape[1]
    tpb = rows_per_batch // tm
    return pl.pallas_call(
        _inproj_body,
        grid=(N // tm, W // tn),
        in_specs=[pl.BlockSpec((tm, D), lambda i, j: (i, 0)),
                  pl.BlockSpec((1, 6, D), lambda i, j: (i // tpb, 0, 0)),
                  pl.BlockSpec((1, D), lambda i, j: (0, 0)),
                  pl.BlockSpec((D, tn), lambda i, j: (0, j)),
                  pl.BlockSpec((1, tn), lambda i, j: (0, j)),
                  pl.BlockSpec((D, WS), lambda i, j: (0, 0)),
                  pl.BlockSpec((1, WS), lambda i, j: (0, 0))],
        out_specs=[pl.BlockSpec((tm, tn), lambda i, j: (i, j)),
                   pl.BlockSpec((tm, WS), lambda i, j: (i, 0))],
        out_shape=[jax.ShapeDtypeStruct((N, W), BF16),
                   jax.ShapeDtypeStruct((N, WS), F32)],
        scratch_shapes=[pltpu.VMEM((tm, D), BF16)],
        compiler_params=_cparams("arbitrary", "arbitrary"),
        name="in_proj",
    )(x2, mod3, g, w_main, b_main, w_side, b_side)


def _attn_body(sinks_ref, q_ref, kc_ref, kp_ref, vc_ref, vp_ref, posc_ref, posp_ref, inv_ref, o_ref):
    n = pl.program_id(1)
    L = ATTN_BLOCK
    inv = inv_ref[...]
    lane = lax.broadcasted_iota(I32, (1, LANES), 1)
    first_half = (lane % ATTN_HEAD_DIM) < (ROPE_DIM // 2)
    low_lanes = lane < ATTN_HEAD_DIM

    ang_c = posc_ref[...] * inv
    ang_p = posp_ref[...] * inv
    cos_c, sin_c = jnp.cos(ang_c), jnp.sin(ang_c)
    cos_p, sin_p = jnp.cos(ang_p), jnp.sin(ang_p)

    def rope(x, cs, sn):
        swapped = jnp.where(first_half, -pltpu.roll(x, LANES - ROPE_DIM // 2, 1), pltpu.roll(x, ROPE_DIM // 2, 1))
        return x * cs + swapped * sn

    k2 = jnp.concatenate([rope(kp_ref[...].astype(F32), cos_p, sin_p),
                          rope(kc_ref[...].astype(F32), cos_c, sin_c)], axis=0)
    v2 = jnp.concatenate([vp_ref[...], vc_ref[...]], axis=0).astype(F32)
    k2s = pltpu.roll(k2, ATTN_HEAD_DIM, 1)
    v2s = pltpu.roll(v2, ATTN_HEAD_DIM, 1)
    zero = jnp.zeros_like(k2)
    k_lo = [jnp.where(low_lanes, k2, zero).astype(BF16), jnp.where(low_lanes, k2s, zero).astype(BF16)]
    k_hi = [jnp.where(low_lanes, zero, k2s).astype(BF16), jnp.where(low_lanes, zero, k2).astype(BF16)]
    v_lo = [jnp.where(low_lanes, v2, zero).astype(BF16), jnp.where(low_lanes, v2s, zero).astype(BF16)]
    v_hi = [jnp.where(low_lanes, zero, v2s).astype(BF16), jnp.where(low_lanes, zero, v2).astype(BF16)]

    qi = lax.broadcasted_iota(I32, (L, 2 * L), 0)
    kj = lax.broadcasted_iota(I32, (L, 2 * L), 1)
    diff = qi + L - kj
    valid = (diff >= 0) & (diff < WINDOW) & ((kj >= L) | (n > 0))

    scale = ATTN_HEAD_DIM ** -0.5
    n_pairs = q_ref.shape[1] // LANES
    for j in range(n_pairs):
        hk = (2 * j) // ATTN_GROUP
        qp = (rope(q_ref[:, j * LANES:(j + 1) * LANES].astype(F32), cos_c, sin_c) * scale).astype(BF16)
        acc = jnp.zeros((L, LANES), F32)
        for half, (kx, vx) in enumerate(((k_lo[hk], v_lo[hk]), (k_hi[hk], v_hi[hk]))):
            s = lax.dot_general(qp, kx, (((1,), (1,)), ((), ())), preferred_element_type=F32)
            s = jnp.where(valid, s, NEG_INF)
            sink = sinks_ref[2 * j + half]
            m = jnp.maximum(jnp.max(s, axis=-1, keepdims=True), sink)
            p = jnp.exp(s - m)
            denom = jnp.sum(p, axis=-1, keepdims=True) + jnp.exp(sink - m)
            acc = acc + jnp.dot(p.astype(BF16), vx, preferred_element_type=F32) / denom
        o_ref[:, j * LANES:(j + 1) * LANES] = acc.astype(BF16)


def _swa_attention(proj, side, pos_f, inv_lane, sinks, B, S, q_width, k_blk, v_blk):
    L = ATTN_BLOCK
    nb = S // L

    def cur(b, n, s):
        return b * nb + n

    def prev(b, n, s):
        return b * nb + jnp.maximum(n - 1, 0)

    grid_spec = pltpu.PrefetchScalarGridSpec(
        num_scalar_prefetch=1,
        grid=(B, nb),
        in_specs=[pl.BlockSpec((L, q_width), lambda b, n, s: (cur(b, n, s), 0)),
                  pl.BlockSpec((L, LANES), lambda b, n, s: (cur(b, n, s), k_blk)),
                  pl.BlockSpec((L, LANES), lambda b, n, s: (prev(b, n, s), k_blk)),
                  pl.BlockSpec((L, LANES), lambda b, n, s: (cur(b, n, s), v_blk)),
                  pl.BlockSpec((L, LANES), lambda b, n, s: (prev(b, n, s), v_blk)),
                  pl.BlockSpec((L, 1), lambda b, n, s: (cur(b, n, s), 0)),
                  pl.BlockSpec((L, 1), lambda b, n, s: (prev(b, n, s), 0)),
                  pl.BlockSpec((1, LANES), lambda b, n, s: (0, 0))],
        out_specs=pl.BlockSpec((L, q_width), lambda b, n, s: (cur(b, n, s), 0)),
    )
    return pl.pallas_call(
        _attn_body,
        grid_spec=grid_spec,
        out_shape=jax.ShapeDtypeStruct((B * S, q_width), BF16),
        compiler_params=_cparams("arbitrary", "arbitrary"),
        name="swa_attn",
    )(sinks, proj, side, side, side, side, pos_f, pos_f, inv_lane)


def _log_sigmoid(z):
    return jnp.minimum(z, 0.0) - jnp.log1p(jnp.exp(-jnp.abs(z)))


def _mlstm_body(qk_ref, v_ref, og_ref, gates_ref, cw_ref, gn_ref, out_ref, hist_ref, c_ref, m_ref):
    L = MLSTM_CHUNK
    H = MLSTM_HEADS
    dqk = qk_ref.shape[1] // (2 * H)
    dv = v_ref.shape[1] // H
    tail = SUBLANES

    @pl.when(pl.program_id(1) == 0)
    def _():
        hist_ref[0:tail, :] = jnp.zeros((tail, hist_ref.shape[1]), F32)
        c_ref[...] = jnp.zeros(c_ref.shape, F32)
        m_ref[...] = jnp.zeros(m_ref.shape, F32)

    hist_ref[tail:tail + L, :] = qk_ref[...].astype(F32)
    conv = hist_ref[tail:tail + L, :] * cw_ref[CONV_WIDTH - 1:CONV_WIDTH, :]
    for t in range(CONV_WIDTH - 1):
        off = tail - (CONV_WIDTH - 1) + t
        conv = conv + hist_ref[off:off + L, :] * cw_ref[t:t + 1, :]
    hist_ref[0:tail, :] = hist_ref[L:L + tail, :]
    qk = conv * _sigmoid(conv)

    gates = gates_ref[...]
    capped = GATE_SOFTCAP * jnp.tanh(gates / GATE_SOFTCAP)
    log_f = _log_sigmoid(capped)
    row = lax.broadcasted_iota(I32, (L, L), 0)
    col = lax.broadcasted_iota(I32, (L, L), 1)
    causal = col <= row
    b_all = jnp.dot(causal.astype(F32), log_f, preferred_element_type=F32, precision=HIGHEST)
    b_all_t = b_all.T
    capped_t = capped.T
    ones_col = jnp.where(lax.broadcasted_iota(I32, (L, LANES), 1) == 0, 1.0, 0.0).astype(BF16)

    for h in range(H):
        b_col = b_all[:, H + h:H + h + 1]
        li_col = capped[:, h:h + 1]
        b_row = b_all_t[H + h:H + h + 1, :]
        li_row = capped_t[h:h + 1, :]
        g_last = b_all[L - 1:L, H + h:H + h + 1]
        m_prev = m_ref[h:h + 1, 0:1]

        d_log = jnp.where(causal, b_col - b_row + li_row, NEG_INF)
        inter = b_col + m_prev
        m = jnp.maximum(inter, jnp.max(d_log, axis=-1, keepdims=True))
        q_h = qk[:, h * dqk:(h + 1) * dqk].astype(BF16)
        k_f = qk[:, (H + h) * dqk:(H + h + 1) * dqk] * (dqk ** -0.5)
        s_qk = lax.dot_general(q_h, k_f.astype(BF16), (((1,), (1,)), ((), ())),
                               preferred_element_type=F32) * jnp.exp(d_log - m)
        v_ext = jnp.concatenate([v_ref[:, h * dv:(h + 1) * dv], ones_col], axis=1)
        c_prev = c_ref[h]
        tot = (jnp.dot(s_qk.astype(BF16), v_ext, preferred_element_type=F32)
               + jnp.exp(inter - m) * jnp.dot(q_h, c_prev.astype(BF16), preferred_element_type=F32))
        h_t = tot[:, :dv] / jnp.maximum(jnp.abs(tot[:, dv:dv + 1]), jnp.exp(-m))

        a_col = g_last - b_col + li_col
        m_new = jnp.maximum(g_last + m_prev, jnp.max(a_col, axis=0, keepdims=True))
        wk = jnp.exp(a_col - m_new) * k_f
        kv = jnp.dot(wk.T.astype(BF16), v_ext, preferred_element_type=F32)
        c_ref[h] = jnp.exp(g_last + m_prev - m_new) * c_prev + kv
        m_ref[h:h + 1, :] = jnp.broadcast_to(m_new, (1, LANES))

        hn = _rms(h_t, gn_ref[:, h * dv:(h + 1) * dv])
        out_ref[:, h * dv:(h + 1) * dv] = (_sigmoid(og_ref[:, h * dv:(h + 1) * dv].astype(F32)) * hn).astype(BF16)


def _mlstm(proj, side, conv_w, g_norm, B, S, qk_blk, v_blk, og_blk, gates_blk, qk_width, v_width):
    L = MLSTM_CHUNK
    nc = S // L
    dv = v_width // MLSTM_HEADS
    dqk = qk_width // (2 * MLSTM_HEADS)

    def rows(b, c):
        return b * nc + c

    return pl.pallas_call(
        _mlstm_body,
        grid=(B, nc),
        in_specs=[pl.BlockSpec((L, qk_width), lambda b, c: (rows(b, c), qk_blk)),
                  pl.BlockSpec((L, v_width), lambda b, c: (rows(b, c), v_blk)),
                  pl.BlockSpec((L, v_width), lambda b, c: (rows(b, c), og_blk)),
                  pl.BlockSpec((L, LANES), lambda b, c: (rows(b, c), gates_blk)),
                  pl.BlockSpec((CONV_WIDTH, qk_width), lambda b, c: (0, 0)),
                  pl.BlockSpec((1, v_width), lambda b, c: (0, 0))],
        out_specs=pl.BlockSpec((L, v_width), lambda b, c: (rows(b, c), 0)),
        out_shape=jax.ShapeDtypeStruct((B * S, v_width), BF16),
        scratch_shapes=[pltpu.VMEM((L + 2 * SUBLANES, qk_width), F32),
                        pltpu.VMEM((MLSTM_HEADS, dqk, dv + LANES), F32),
                        pltpu.VMEM((SUBLANES, LANES), F32)],
        compiler_params=_cparams("arbitrary", "arbitrary"),
        name="mlstm",
    )(proj, proj, proj, side, conv_w, g_norm)


def _merge_body(a_ref, m_ref, ga_ref, gm_ref, wa_ref, wm_ref, o_ref):
    ya = jnp.dot(a_ref[...], wa_ref[...], preferred_element_type=F32)
    ym = jnp.dot(m_ref[...], wm_ref[...], preferred_element_type=F32)
    merged = _sigmoid(ga_ref[...].astype(F32)) * ya + _sigmoid(gm_ref[...].astype(F32)) * ym
    o_ref[...] = merged.astype(BF16)


def _merge(attn_o, mls_o, proj, wa, wm, ga_blk, gm_blk, tm=1024):
    N = attn_o.shape[0]
    D = wa.shape[1]
    wa_w, wm_w = attn_o.shape[1], mls_o.shape[1]
    const = lambda i: (0, 0)
    return pl.pallas_call(
        _merge_body,
        grid=(N // tm,),
        in_specs=[pl.BlockSpec((tm, wa_w), lambda i: (i, 0)),
                  pl.BlockSpec((tm, wm_w), lambda i: (i, 0)),
                  pl.BlockSpec((tm, D), lambda i: (i, ga_blk)),
                  pl.BlockSpec((tm, D), lambda i: (i, gm_blk)),
                  pl.BlockSpec((wa_w, D), const),
                  pl.BlockSpec((wm_w, D), const)],
        out_specs=pl.BlockSpec((tm, D), lambda i: (i, 0)),
        out_shape=jax.ShapeDtypeStruct((N, D), BF16),
        compiler_params=_cparams("arbitrary"),
        name="merge",
    )(attn_o, mls_o, proj, proj, wa, wm)


def _split_bf16(v):
    hi = v.astype(BF16)
    return hi, (v - hi.astype(F32)).astype(BF16)


def _outproj_body(mg_ref, x_ref, mod_ref, gpost_ref, gpre_ref, wo_ref, wr_ref, br_ref, x1_ref, h2_ref, lg_ref):
    y = jnp.dot(mg_ref[...], wo_ref[...], preferred_element_type=F32)
    x1 = x_ref[...] + mod_ref[0, 2:3, :] * _rms(y, gpost_ref[...])
    x1_ref[...] = x1
    h2 = _rms(x1, gpre_ref[...]) * (1.0 + mod_ref[0, 4:5, :]) + mod_ref[0, 3:4, :]
    h2_ref[...] = h2
    h_hi, h_lo = _split_bf16(h2)
    w_hi, w_lo = _split_bf16(wr_ref[...])
    lg = (jnp.dot(h_hi, w_hi, preferred_element_type=F32) + jnp.dot(h_hi, w_lo, preferred_element_type=F32)
          + jnp.dot(h_lo, w_hi, preferred_element_type=F32))
    lg_ref[...] = lg + br_ref[...]


def _out_proj(merged, x2, mod3, g_post, g_pre, wo, wr, br, rows_per_batch, tm=512):
    N, D = x2.shape
    tpb = rows_per_batch // tm
    const = lambda i: (0, 0)
    return pl.pallas_call(
        _outproj_body,
        grid=(N // tm,),
        in_specs=[pl.BlockSpec((tm, D), lambda i: (i, 0)),
                  pl.BlockSpec((tm, D), lambda i: (i, 0)),
                  pl.BlockSpec((1, 6, D), lambda i: (i // tpb, 0, 0)),
                  pl.BlockSpec((1, D), const),
                  pl.BlockSpec((1, D), const),
                  pl.BlockSpec((D, D), const),
                  pl.BlockSpec((D, LANES), const),
                  pl.BlockSpec((1, LANES), const)],
        out_specs=[pl.BlockSpec((tm, D), lambda i: (i, 0)),
                   pl.BlockSpec((tm, D), lambda i: (i, 0)),
                   pl.BlockSpec((tm, LANES), lambda i: (i, 0))],
        out_shape=[jax.ShapeDtypeStruct((N, D), F32),
                   jax.ShapeDtypeStruct((N, D), F32),
                   jax.ShapeDtypeStruct((N, LANES), F32)],
        compiler_params=_cparams("arbitrary"),
        name="out_proj",
    )(merged, x2, mod3, g_post, g_pre, wo, wr, br)


def _route_body(lg_ref, e_ref, w_ref, r_ref, cnt_ref, run_ref):
    T = lg_ref.shape[0]
    E = N_EXPERTS

    @pl.when(pl.program_id(0) == 0)
    def _():
        run_ref[...] = jnp.zeros(run_ref.shape, F32)

    lg = lg_ref[...].T[0:E, :]
    eidx = lax.broadcasted_iota(I32, (E, T), 0)
    tops, idxs, hots = [], [], []
    for _ in range(TOP_K):
        mx = jnp.max(lg, axis=0, keepdims=True)
        ik = jnp.min(jnp.where(lg == mx, eidx, E), axis=0, keepdims=True)
        hot = eidx == ik
        lg = jnp.where(hot, NEG_INF, lg)
        tops.append(mx)
        idxs.append(ik)
        hots.append(hot)
    exps = [jnp.exp(t - tops[0]) for t in tops]
    tot = exps[0] + exps[1] + exps[2] + exps[3]

    sel = jnp.zeros((E, T), F32)
    for hot in hots:
        sel = sel + jnp.where(hot, 1.0, 0.0)
    before = lax.broadcasted_iota(I32, (T, T), 0) < lax.broadcasted_iota(I32, (T, T), 1)
    excl = jnp.dot(sel.astype(BF16), jnp.where(before, 1.0, 0.0).astype(BF16), preferred_element_type=F32)
    base = excl + run_ref[:, 0:1]
    for k in range(TOP_K):
        e_ref[k:k + 1, :] = idxs[k]
        w_ref[k:k + 1, :] = exps[k] / tot
        r_ref[k:k + 1, :] = jnp.sum(jnp.where(hots[k], base, 0.0), axis=0, keepdims=True).astype(I32)
    run_ref[...] = run_ref[...] + jnp.sum(sel, axis=1, keepdims=True)
    cnt_ref[...] = run_ref[...]


def _route(logits, T=512):
    N = logits.shape[0]
    return pl.pallas_call(
        _route_body,
        grid=(N // T,),
        in_specs=[pl.BlockSpec((T, LANES), lambda i: (i, 0))],
        out_specs=[pl.BlockSpec((TOP_K, T), lambda i: (0, i)),
                   pl.BlockSpec((TOP_K, T), lambda i: (0, i)),
                   pl.BlockSpec((TOP_K, T), lambda i: (0, i)),
                   pl.BlockSpec((N_EXPERTS, LANES), lambda i: (0, 0))],
        out_shape=[jax.ShapeDtypeStruct((TOP_K, N), I32),
                   jax.ShapeDtypeStruct((TOP_K, N), F32),
                   jax.ShapeDtypeStruct((TOP_K, N), I32),
                   jax.ShapeDtypeStruct((N_EXPERTS, LANES), F32)],
        scratch_shapes=[pltpu.VMEM((N_EXPERTS, LANES), F32)],
        compiler_params=_cparams("arbitrary"),
        name="route",
    )(logits)


def _dest_body(e_ref, r_ref, ps_ref, d_ref):
    T = e_ref.shape[1]
    eidx = lax.broadcasted_iota(I32, (N_EXPERTS, T), 0)
    starts = ps_ref[:, 0:1]
    for k in range(TOP_K):
        base = jnp.sum(jnp.where(eidx == e_ref[k:k + 1, :], starts, 0.0), axis=0, keepdims=True)
        d_ref[k:k + 1, :] = r_ref[k:k + 1, :] + base.astype(I32)


def _dest_rows(top_e, rank, pad_starts_f, T=2048):
    N = top_e.shape[1]
    return pl.pallas_call(
        _dest_body,
        grid=(N // T,),
        in_specs=[pl.BlockSpec((TOP_K, T), lambda i: (0, i)),
                  pl.BlockSpec((TOP_K, T), lambda i: (0, i)),
                  pl.BlockSpec((N_EXPERTS, LANES), lambda i: (0, 0))],
        out_specs=pl.BlockSpec((TOP_K, T), lambda i: (0, i)),
        out_shape=jax.ShapeDtypeStruct((TOP_K, N), I32),
        compiler_params=_cparams("arbitrary"),
        name="dest_rows",
    )(top_e, rank, pad_starts_f)


def _dispatch_body(zf_ref, dest_ref, h_ref, xs_hbm, zeros_ref, sem, zsem):
    tc = h_ref.shape[0]
    tm = zeros_ref.shape[0]

    @pl.when(pl.program_id(0) == 0)
    def _():
        zeros_ref[...] = jnp.zeros(zeros_ref.shape, zeros_ref.dtype)

        def tile_copy(i):
            return pltpu.make_async_copy(zeros_ref, xs_hbm.at[pl.ds(pl.multiple_of(i * tm, tm), tm), :], zsem)

        def zero_issue(i, carry):
            @pl.when(zf_ref[i] != 0)
            def _():
                tile_copy(i).start()
            return carry

        def zero_drain(i, carry):
            @pl.when(zf_ref[i] != 0)
            def _():
                tile_copy(i).wait()
            return carry

        lax.fori_loop(0, zf_ref.shape[0], zero_issue, 0)
        lax.fori_loop(0, zf_ref.shape[0], zero_drain, 0)

    def row_copy(t, k):
        return pltpu.make_async_copy(h_ref.at[pl.ds(t, 1), :], xs_hbm.at[pl.ds(dest_ref[k, t], 1), :], sem)

    def issue(t, carry):
        for k in range(TOP_K):
            row_copy(t, k).start()
        return carry

    def drain(t, carry):
        for k in range(TOP_K):
            row_copy(t, k).wait()
        return carry

    lax.fori_loop(0, tc, issue, 0)
    lax.fori_loop(0, tc, drain, 0)


def _dispatch(zero_tiles, dest, h2, n_rows, tm, tc=256):
    N, D = h2.shape
    grid_spec = pltpu.PrefetchScalarGridSpec(
        num_scalar_prefetch=1,
        grid=(N // tc,),
        in_specs=[pl.BlockSpec((TOP_K, tc), lambda i, zf: (0, i), memory_space=pltpu.SMEM),
                  pl.BlockSpec((tc, D), lambda i, zf: (i, 0))],
        out_specs=pl.BlockSpec(memory_space=pl.ANY),
        scratch_shapes=[pltpu.VMEM((tm, D), h2.dtype),
                        pltpu.SemaphoreType.DMA(()),
                        pltpu.SemaphoreType.DMA(())],
    )
    return pl.pallas_call(
        _dispatch_body,
        grid_spec=grid_spec,
        out_shape=jax.ShapeDtypeStruct((n_rows, D), h2.dtype),
        compiler_params=_cparams("arbitrary"),
        name="dispatch",
    )(zero_tiles, dest, h2)


PAIR_CHUNK = 2 * LANES
UP_CHUNK = 1024


def _new_weights(te_ref, i):
    return (i == 0) | (te_ref[i] != te_ref[jnp.maximum(i - 1, 0)])


def _ffn_up_body(te_ref, nv_ref, x_ref, w_ref, b_ref, sel_ref, o_ref, wb_ref):
    i = pl.program_id(1)

    @pl.when((i < nv_ref[0]) & _new_weights(te_ref, i))
    def _():
        wb_ref[...] = w_ref[0].astype(BF16)

    @pl.when(i < nv_ref[0])
    def _():
        xb = x_ref[...].astype(BF16)
        for c in range(wb_ref.shape[1] // UP_CHUNK):
            cols = slice(c * UP_CHUNK, (c + 1) * UP_CHUNK)
            gu = jnp.dot(xb, wb_ref[:, cols], preferred_element_type=F32) + b_ref[0, :, cols]
            glu = jnp.minimum(gu, SWIGLU_LIMIT)
            half = 0.5 * glu
            gate = half + half * jnp.tanh(SWIGLU_ALPHA * half)
            lin = jnp.clip(gu, -SWIGLU_LIMIT, SWIGLU_LIMIT) + 1.0
            for p in range(UP_CHUNK // PAIR_CHUNK):
                parts = []
                for v in range(PAIR_CHUNK // LANES):
                    lo = p * PAIR_CHUNK + v * LANES
                    parts.append(gate[:, lo:lo + LANES] * pltpu.roll(lin[:, lo:lo + LANES], LANES - 1, 1))
                prod = jnp.concatenate(parts, axis=1).astype(BF16)
                out_lo = (c * UP_CHUNK + p * PAIR_CHUNK) // 2
                o_ref[:, out_lo:out_lo + LANES] = jnp.dot(prod, sel_ref[...], preferred_element_type=F32).astype(BF16)

    @pl.when(i >= nv_ref[0])
    def _():
        o_ref[...] = jnp.zeros(o_ref.shape, o_ref.dtype)


def _ffn_up(tile_e, n_valid, xs, w_gate_up, b_gate_up, tm, tf=1024):
    P, D = xs.shape
    E, _, F2 = w_gate_up.shape
    F = F2 // 2
    sel = (jnp.arange(PAIR_CHUNK)[:, None] == 2 * jnp.arange(LANES)[None, :]).astype(BF16)

    def row(i, nv):
        return jnp.minimum(i, nv[0] - 1)

    grid_spec = pltpu.PrefetchScalarGridSpec(
        num_scalar_prefetch=2,
        grid=(F // tf, P // tm),
        in_specs=[pl.BlockSpec((tm, D), lambda j, i, te, nv: (row(i, nv), 0)),
                  pl.BlockSpec((1, D, 2 * tf), lambda j, i, te, nv: (te[row(i, nv)], 0, j)),
                  pl.BlockSpec((1, 1, 2 * tf), lambda j, i, te, nv: (te[row(i, nv)], 0, j)),
                  pl.BlockSpec((PAIR_CHUNK, LANES), lambda j, i, te, nv: (0, 0))],
        out_specs=pl.BlockSpec((tm, tf), lambda j, i, te, nv: (i, j)),
        scratch_shapes=[pltpu.VMEM((D, 2 * tf), BF16)],
    )
    return pl.pallas_call(
        _ffn_up_body,
        grid_spec=grid_spec,
        out_shape=jax.ShapeDtypeStruct((P, F), BF16),
        compiler_params=_cparams("arbitrary", "arbitrary"),
        name="ffn_up",
    )(tile_e, n_valid, xs, w_gate_up, b_gate_up.reshape(E, 1, F2), sel)


def _ffn_down_body(te_ref, nv_ref, a_ref, w_ref, b_ref, o_ref, wb_ref):
    i = pl.program_id(1)

    @pl.when((i < nv_ref[0]) & _new_weights(te_ref, i))
    def _():
        wb_ref[...] = w_ref[0].astype(BF16)

    @pl.when(i < nv_ref[0])
    def _():
        o_ref[...] = jnp.dot(a_ref[...], wb_ref[...], preferred_element_type=F32) + b_ref[0]

    @pl.when(i >= nv_ref[0])
    def _():
        o_ref[...] = jnp.zeros(o_ref.shape, o_ref.dtype)


def _ffn_down(tile_e, n_valid, act, w_down, b_down, tm, tn=2048):
    P, F = act.shape
    E, _, D = w_down.shape

    def row(i, nv):
        return jnp.minimum(i, nv[0] - 1)

    grid_spec = pltpu.PrefetchScalarGridSpec(
        num_scalar_prefetch=2,
        grid=(D // tn, P // tm),
        in_specs=[pl.BlockSpec((tm, F), lambda j, i, te, nv: (row(i, nv), 0)),
                  pl.BlockSpec((1, F, tn), lambda j, i, te, nv: (te[row(i, nv)], 0, j)),
                  pl.BlockSpec((1, 1, tn), lambda j, i, te, nv: (te[row(i, nv)], 0, j))],
        out_specs=pl.BlockSpec((tm, tn), lambda j, i, te, nv: (i, j)),
        scratch_shapes=[pltpu.VMEM((F, tn), BF16)],
    )
    return pl.pallas_call(
        _ffn_down_body,
        grid_spec=grid_spec,
        out_shape=jax.ShapeDtypeStruct((P, D), F32),
        compiler_params=_cparams("arbitrary", "arbitrary"),
        name="ffn_down",
    )(tile_e, n_valid, act, w_down, b_down.reshape(E, 1, D))


def _combine_body(dcur_ref, dnxt_ref, w_ref, x1_ref, mod_ref, g_ref, y_hbm, o_ref, buf_ref, sem_ref):
    i = pl.program_id(0)
    n = pl.num_programs(0)
    tt = x1_ref.shape[0]
    slot = i % 2

    def row_copy(d_ref, s, t, k):
        return pltpu.make_async_copy(y_hbm.at[pl.ds(d_ref[k, t], 1), :],
                                     buf_ref.at[s, k, pl.ds(t, 1), :], sem_ref.at[s])

    def issue(d_ref, s):
        def body(t, carry):
            for k in range(TOP_K):
                row_copy(d_ref, s, t, k).start()
            return carry
        lax.fori_loop(0, tt, body, 0)

    @pl.when(i == 0)
    def _():
        issue(dcur_ref, 0)

    @pl.when(i + 1 < n)
    def _():
        issue(dnxt_ref, 1 - slot)

    def drain(t, carry):
        for k in range(TOP_K):
            row_copy(dcur_ref, slot, t, k).wait()
        return carry
    lax.fori_loop(0, tt, drain, 0)

    acc = jnp.zeros(x1_ref.shape, F32)
    for k in range(TOP_K):
        acc = acc + w_ref[:, k:k + 1] * buf_ref[slot, k]
    o_ref[...] = x1_ref[...] + mod_ref[0, 5:6, :] * _rms(acc, g_ref[...])


def _combine(dest, top_w_t, x1, mod3, g_post, y_rows, rows_per_batch, tt=128):
    N, D = x1.shape
    nt = N // tt
    tpb = rows_per_batch // tt
    return pl.pallas_call(
        _combine_body,
        grid=(nt,),
        in_specs=[pl.BlockSpec((TOP_K, tt), lambda i: (0, i), memory_space=pltpu.SMEM),
                  pl.BlockSpec((TOP_K, tt), lambda i: (0, jnp.minimum(i + 1, nt - 1)), memory_space=pltpu.SMEM),
                  pl.BlockSpec((tt, TOP_K), lambda i: (i, 0)),
                  pl.BlockSpec((tt, D), lambda i: (i, 0)),
                  pl.BlockSpec((1, 6, D), lambda i: (i // tpb, 0, 0)),
                  pl.BlockSpec((1, D), lambda i: (0, 0)),
                  pl.BlockSpec(memory_space=pl.ANY)],
        out_specs=pl.BlockSpec((tt, D), lambda i: (i, 0)),
        out_shape=jax.ShapeDtypeStruct((N, D), F32),
        scratch_shapes=[pltpu.VMEM((2, TOP_K, tt, D), F32),
                        pltpu.SemaphoreType.DMA((2,))],
        compiler_params=_cparams("arbitrary"),
        name="combine",
    )(dest, dest, top_w_t, x1, mod3, g_post, y_rows)


def _block(x, mod3, positions, g_pre_mix, g_post_mix, g_pre_ffn, g_post_ffn, w_in, b_in, conv_qk, sinks, g_mlstm,
           w_attn_proj, w_mlstm_proj, w_out, w_router, b_router, w_gate_up, b_gate_up, w_down, b_down, ffn_tm=512):
    B, S, D = x.shape
    N = B * S
    aq_w = w_attn_proj.shape[0]
    akv_w = (aq_w // ATTN_HEAD_DIM // ATTN_GROUP) * ATTN_HEAD_DIM
    mv_w = w_mlstm_proj.shape[0]
    mqk_w = mv_w // 2
    H = MLSTM_HEADS
    sizes = (aq_w, akv_w, akv_w, mqk_w, mqk_w, mv_w, mv_w, H, H, D, D)
    offs = [0]
    for sz in sizes:
        offs.append(offs[-1] + sz)
    seg_w = lambda k: w_in[:, offs[k]:offs[k + 1]]
    seg_b = lambda k: b_in[offs[k]:offs[k + 1]]
    order = (0, 3, 4, 5, 6, 9, 10)
    w_main = jnp.concatenate([seg_w(k) for k in order], axis=1).astype(BF16)
    b_main = jnp.concatenate([seg_b(k) for k in order]).reshape(1, -1)
    pad = LANES - 2 * H
    w_side = jnp.concatenate([seg_w(1), seg_w(2), seg_w(7), seg_w(8), jnp.zeros((D, pad), F32)], axis=1).astype(BF16)
    b_side = jnp.concatenate([seg_b(1), seg_b(2), seg_b(7), seg_b(8), jnp.zeros((pad,), F32)]).reshape(1, -1)

    x2 = x.reshape(N, D)
    proj, side = _in_proj(x2, mod3, g_pre_mix.reshape(1, D), w_main, b_main, w_side, b_side, S)

    c_mq = aq_w
    c_mv = aq_w + 2 * mqk_w
    c_mo = c_mv + mv_w
    c_ga = c_mo + mv_w
    c_gm = c_ga + D
    side_k, side_v, side_gates = 0, akv_w // LANES, 2 * akv_w // LANES

    half = ROPE_DIM // 2
    inv_freq = ROPE_THETA ** (-jnp.arange(half, dtype=F32) / half)
    lane_d = jnp.arange(LANES) % ATTN_HEAD_DIM
    inv_lane = jnp.where(lane_d < ROPE_DIM, inv_freq[lane_d % half], 0.0).reshape(1, LANES).astype(F32)
    pos_f = positions.astype(F32).reshape(N, 1)
    attn_o = _swa_attention(proj, side, pos_f, inv_lane, sinks, B, S, aq_w, side_k, side_v)

    mls_o = _mlstm(proj, side, conv_qk, g_mlstm.reshape(1, mv_w), B, S,
                   c_mq // (2 * mqk_w), c_mv // mv_w, c_mo // mv_w, side_gates, 2 * mqk_w, mv_w)

    E = N_EXPERTS
    wr = jnp.concatenate([w_router, jnp.zeros((D, LANES - E), F32)], axis=1)
    br = jnp.concatenate([b_router, jnp.full((LANES - E,), -1e30, F32)]).reshape(1, LANES)
    merged = _merge(attn_o, mls_o, proj, w_attn_proj.astype(BF16), w_mlstm_proj.astype(BF16), c_ga // D, c_gm // D)
    x1, h2, logits = _out_proj(merged, x2, mod3, g_post_mix.reshape(1, D), g_pre_ffn.reshape(1, D),
                               w_out.astype(BF16), wr, br, S)

    top_e, top_w, rank, counts = _route(logits)
    cnt = counts[:, 0].astype(I32)
    padded = (cnt + ffn_tm - 1) // ffn_tm * ffn_tm
    pad_ends = jnp.cumsum(padded)
    pad_starts = pad_ends - padded
    n_rows = N * TOP_K + E * ffn_tm
    n_tiles = n_rows // ffn_tm
    tile_start = jnp.arange(n_tiles, dtype=I32) * ffn_tm
    tile_e = jnp.minimum(jnp.sum((pad_ends[None, :] <= tile_start[:, None]).astype(I32), axis=1), E - 1)
    n_valid = (pad_ends[E - 1:E] // ffn_tm).astype(I32)
    dest = _dest_rows(top_e, rank, jnp.broadcast_to(pad_starts.astype(F32)[:, None], (E, LANES)))

    tile_end = tile_start + ffn_tm
    zero_tiles = (jnp.any(pad_ends[None, :] == tile_end[:, None], axis=1)
                  | (tile_start >= pad_ends[E - 1])).astype(I32)
    xs = _dispatch(zero_tiles, dest, h2, n_rows, ffn_tm)
    act = _ffn_up(tile_e, n_valid, xs, w_gate_up, b_gate_up, ffn_tm)
    y_rows = _ffn_down(tile_e, n_valid, act, w_down, b_down, ffn_tm)
    out = _combine(dest, top_w.T, x1, mod3, g_post_ffn.reshape(1, D), y_rows, S)
    return out.reshape(B, S, D)


def kernel(x, c, positions, w_ada, b_ada, g_pre_mix, g_post_mix, g_pre_ffn, g_post_ffn, w_in, b_in, conv_qk, sinks,
           g_mlstm, w_attn_proj, w_mlstm_proj, w_out, w_router, b_router, w_gate_up, b_gate_up, w_down, b_down):
    B, S, D = x.shape
    depth = w_ada.shape[0]
    for l in range(depth):
        mod3 = _ada_ln(c, w_ada[l], b_ada[l]).reshape(B, 6, D)
        x = _block(x, mod3, positions, g_pre_mix[l], g_post_mix[l], g_pre_ffn[l], g_post_ffn[l], w_in[l], b_in[l],
                   conv_qk[l], sinks[l], g_mlstm[l], w_attn_proj[l], w_mlstm_proj[l], w_out[l], w_router[l],
                   b_router[l], w_gate_up[l], b_gate_up[l], w_down[l], b_down[l])
    return x
```

```python
import jax
import jax.numpy as jnp
from jax import lax
from jax.experimental import pallas as pl
from jax.experimental.pallas import tpu as pltpu

F32 = jnp.float32
BF16 = jnp.bfloat16
I32 = jnp.int32
HIGHEST = lax.Precision.HIGHEST

ATTN_HEAD_DIM = 64
ATTN_GROUP = 8
WINDOW = 128
ATTN_BLOCK = 128
ROPE_THETA = 500000.0
ROPE_DIM = ATTN_HEAD_DIM // 4
MLSTM_HEADS = 4
MLSTM_CHUNK = 128
CONV_WIDTH = 4
GATE_SOFTCAP = 15.0
N_EXPERTS = 32
TOP_K = 4
SWIGLU_ALPHA = 1.702
SWIGLU_LIMIT = 7.0
NORM_EPS = 1e-6

LANES = 128
SUBLANES = 8
VMEM_LIMIT = 60 * 1024 * 1024

NEG_INF = float("-inf")


def _cparams(*sem):
    return pltpu.CompilerParams(dimension_semantics=sem, vmem_limit_bytes=VMEM_LIMIT)


def _rms(x, g):
    ms = jnp.mean(x * x, axis=-1, keepdims=True)
    return x * lax.rsqrt(ms + NORM_EPS) * g


def _sigmoid(x):
    return 1.0 / (1.0 + jnp.exp(-x))


def _ada_body(c_ref, w_ref, b_ref, o_ref):
    c = c_ref[...]
    cond = c * _sigmoid(c)
    o_ref[...] = jnp.dot(cond, w_ref[...], preferred_element_type=F32, precision=HIGHEST) + b_ref[...]


def _ada_ln(c, w_ada, b_ada):
    B, D = c.shape
    W = w_ada.shape[1]
    tn = 1024
    cp = jnp.zeros((SUBLANES, D), F32).at[:B].set(c)
    out = pl.pallas_call(
        _ada_body,
        grid=(W // tn,),
        in_specs=[pl.BlockSpec((SUBLANES, D), lambda j: (0, 0)),
                  pl.BlockSpec((D, tn), lambda j: (0, j)),
                  pl.BlockSpec((1, tn), lambda j: (0, j))],
        out_specs=pl.BlockSpec((SUBLANES, tn), lambda j: (0, j)),
        out_shape=jax.ShapeDtypeStruct((SUBLANES, W), F32),
        compiler_params=_cparams("arbitrary"),
        name="ada_ln",
    )(cp, w_ada, b_ada.reshape(1, W))
    return out[:B]


def _inproj_body(x_ref, mod_ref, g_ref, w_ref, b_ref, wg_ref, bg_ref, o_ref, og_ref, h_ref):
    @pl.when(pl.program_id(1) == 0)
    def _():
        h = _rms(x_ref[...], g_ref[...]) * (1.0 + mod_ref[0, 1:2, :]) + mod_ref[0, 0:1, :]
        hb = h.astype(BF16)
        h_ref[...] = hb
        og_ref[...] = jnp.dot(hb, wg_ref[...], preferred_element_type=F32) + bg_ref[...]

    acc = jnp.dot(h_ref[...], w_ref[...], preferred_element_type=F32) + b_ref[...]
    o_ref[...] = acc.astype(BF16)


def _in_proj(x2, mod3, g, w_main, b_main, w_side, b_side, rows_per_batch, tm=1024, tn=1024):
    N, D = x2.shape
    W = w_main.shape[1]
    WS = w_side.shape[1]
    tpb = rows_per_batch // tm
    return pl.pallas_call(
        _inproj_body,
        grid=(N // tm, W // tn),
        in_specs=[pl.BlockSpec((tm, D), lambda i, j: (i, 0)),
                  pl.BlockSpec((1, 6, D), lambda i, j: (i // tpb, 0, 0)),
                  pl.BlockSpec((1, D), lambda i, j: (0, 0)),
                  pl.BlockSpec((D, tn), lambda i, j: (0, j)),
                  pl.BlockSpec((1, tn), lambda i, j: (0, j)),
                  pl.BlockSpec((D, WS), lambda i, j: (0, 0)),
                  pl.BlockSpec((1, WS), lambda i, j: (0, 0))],
        out_specs=[pl.BlockSpec((tm, tn), lambda i, j: (i, j)),
                   pl.BlockSpec((tm, WS), lambda i, j: (i, 0))],
        out_shape=[jax.ShapeDtypeStruct((N, W), BF16),
                   jax.ShapeDtypeStruct((N, WS), F32)],
        scratch_shapes=[pltpu.VMEM((tm, D), BF16)],
        compiler_params=_cparams("arbitrary", "arbitrary"),
        name="in_proj",
    )(x2, mod3, g, w_main, b_main, w_side, b_side)


def _attn_compute(sinks_ref, q_ref, kc_ref, kp_ref, vc_ref, vp_ref, cos_c, sin_c, cos_p, sin_p, o_ref):
    n = pl.program_id(1)
    L = ATTN_BLOCK
    lane = lax.broadcasted_iota(I32, (1, LANES), 1)
    first_half = (lane % ATTN_HEAD_DIM) < (ROPE_DIM // 2)
    low_lanes = lane < ATTN_HEAD_DIM

    def rope(x, cs, sn):
        swapped = jnp.where(first_half, -pltpu.roll(x, LANES - ROPE_DIM // 2, 1), pltpu.roll(x, ROPE_DIM // 2, 1))
        return x * cs + swapped * sn

    k2 = jnp.concatenate([rope(kp_ref[...].astype(F32), cos_p, sin_p),
                          rope(kc_ref[...].astype(F32), cos_c, sin_c)], axis=0)
    v2 = jnp.concatenate([vp_ref[...], vc_ref[...]], axis=0).astype(F32)
    k2s = pltpu.roll(k2, ATTN_HEAD_DIM, 1)
    v2s = pltpu.roll(v2, ATTN_HEAD_DIM, 1)
    zero = jnp.zeros_like(k2)
    k_lo = [jnp.where(low_lanes, k2, zero).astype(BF16), jnp.where(low_lanes, k2s, zero).astype(BF16)]
    k_hi = [jnp.where(low_lanes, zero, k2s).astype(BF16), jnp.where(low_lanes, zero, k2).astype(BF16)]
    v_lo = [jnp.where(low_lanes, v2, zero).astype(BF16), jnp.where(low_lanes, v2s, zero).astype(BF16)]
    v_hi = [jnp.where(low_lanes, zero, v2s).astype(BF16), jnp.where(low_lanes, zero, v2).astype(BF16)]

    qi = lax.broadcasted_iota(I32, (L, 2 * L), 0)
    kj = lax.broadcasted_iota(I32, (L, 2 * L), 1)
    diff = qi + L - kj
    valid = (diff >= 0) & (diff < WINDOW) & ((kj >= L) | (n > 0))

    scale = ATTN_HEAD_DIM ** -0.5
    n_pairs = q_ref.shape[1] // LANES
    for j in range(n_pairs):
        hk = (2 * j) // ATTN_GROUP
        qp = (rope(q_ref[:, j * LANES:(j + 1) * LANES].astype(F32), cos_c, sin_c) * scale).astype(BF16)
        acc = jnp.zeros((L, LANES), F32)
        for half, (kx, vx) in enumerate(((k_lo[hk], v_lo[hk]), (k_hi[hk], v_hi[hk]))):
            s = lax.dot_general(qp, kx, (((1,), (1,)), ((), ())), preferred_element_type=F32)
            s = jnp.where(valid, s, NEG_INF)
            sink = sinks_ref[2 * j + half]
            m = jnp.maximum(jnp.max(s, axis=-1, keepdims=True), sink)
            p = jnp.exp(s - m)
            denom = jnp.sum(p, axis=-1, keepdims=True) + jnp.exp(sink - m)
            acc = acc + jnp.dot(p.astype(BF16), vx, preferred_element_type=F32) / denom
        o_ref[:, j * LANES:(j + 1) * LANES] = acc.astype(BF16)


def _log_sigmoid(z):
    return jnp.minimum(z, 0.0) - jnp.log1p(jnp.exp(-jnp.abs(z)))


def _mlstm_compute(qk_ref, v_ref, og_ref, gates_ref, cw_ref, gn_ref, out_ref, hist_ref, c_ref, m_ref):
    L = MLSTM_CHUNK
    H = MLSTM_HEADS
    dqk = qk_ref.shape[1] // (2 * H)
    dv = v_ref.shape[1] // H
    tail = SUBLANES

    hist_ref[tail:tail + L, :] = qk_ref[...].astype(F32)
    conv = hist_ref[tail:tail + L, :] * cw_ref[CONV_WIDTH - 1:CONV_WIDTH, :]
    for t in range(CONV_WIDTH - 1):
        off = tail - (CONV_WIDTH - 1) + t
        conv = conv + hist_ref[off:off + L, :] * cw_ref[t:t + 1, :]
    hist_ref[0:tail, :] = hist_ref[L:L + tail, :]
    qk = conv * _sigmoid(conv)

    gates = gates_ref[...]
    capped = GATE_SOFTCAP * jnp.tanh(gates / GATE_SOFTCAP)
    log_f = _log_sigmoid(capped)
    row = lax.broadcasted_iota(I32, (L, L), 0)
    col = lax.broadcasted_iota(I32, (L, L), 1)
    causal = col <= row
    b_all = jnp.dot(causal.astype(F32), log_f, preferred_element_type=F32, precision=HIGHEST)
    b_all_t = b_all.T
    capped_t = capped.T
    ones_col = jnp.where(lax.broadcasted_iota(I32, (L, LANES), 1) == 0, 1.0, 0.0).astype(BF16)

    for h in range(H):
        b_col = b_all[:, H + h:H + h + 1]
        li_col = capped[:, h:h + 1]
        b_row = b_all_t[H + h:H + h + 1, :]
        li_row = capped_t[h:h + 1, :]
        g_last = b_all[L - 1:L, H + h:H + h + 1]
        m_prev = m_ref[h:h + 1, 0:1]

        d_log = jnp.where(causal, b_col - b_row + li_row, NEG_INF)
        inter = b_col + m_prev
        m = jnp.maximum(inter, jnp.max(d_log, axis=-1, keepdims=True))
        q_h = qk[:, h * dqk:(h + 1) * dqk].astype(BF16)
        k_f = qk[:, (H + h) * dqk:(H + h + 1) * dqk] * (dqk ** -0.5)
        s_qk = lax.dot_general(q_h, k_f.astype(BF16), (((1,), (1,)), ((), ())),
                               preferred_element_type=F32) * jnp.exp(d_log - m)
        v_ext = jnp.concatenate([v_ref[:, h * dv:(h + 1) * dv], ones_col], axis=1)
        c_prev = c_ref[h]
        tot = (jnp.dot(s_qk.astype(BF16), v_ext, preferred_element_type=F32)
               + jnp.exp(inter - m) * jnp.dot(q_h, c_prev.astype(BF16), preferred_element_type=F32))
        h_t = tot[:, :dv] / jnp.maximum(jnp.abs(tot[:, dv:dv + 1]), jnp.exp(-m))

        a_col = g_last - b_col + li_col
        m_new = jnp.maximum(g_last + m_prev, jnp.max(a_col, axis=0, keepdims=True))
        wk = jnp.exp(a_col - m_new) * k_f
        kv = jnp.dot(wk.T.astype(BF16), v_ext, preferred_element_type=F32)
        c_ref[h] = jnp.exp(g_last + m_prev - m_new) * c_prev + kv
        m_ref[h:h + 1, :] = jnp.broadcast_to(m_new, (1, LANES))

        hn = _rms(h_t, gn_ref[:, h * dv:(h + 1) * dv])
        out_ref[:, h * dv:(h + 1) * dv] = (_sigmoid(og_ref[:, h * dv:(h + 1) * dv].astype(F32)) * hn).astype(BF16)


def _mixers_body(sinks_ref, q_ref, kc_ref, kp_ref, vc_ref, vp_ref, pos_ref, inv_ref,
                 qk_ref, v_ref, og_ref, gates_ref, cw_ref, gn_ref,
                 attn_ref, mls_ref, trig_ref, hist_ref, c_ref, m_ref):
    ang = pos_ref[...] * inv_ref[...]
    cos_c, sin_c = jnp.cos(ang), jnp.sin(ang)

    @pl.when(pl.program_id(1) == 0)
    def _():
        trig_ref[0] = cos_c
        trig_ref[1] = sin_c
        hist_ref[0:SUBLANES, :] = jnp.zeros((SUBLANES, hist_ref.shape[1]), F32)
        c_ref[...] = jnp.zeros(c_ref.shape, F32)
        m_ref[...] = jnp.zeros(m_ref.shape, F32)

    cos_p, sin_p = trig_ref[0], trig_ref[1]
    _attn_compute(sinks_ref, q_ref, kc_ref, kp_ref, vc_ref, vp_ref, cos_c, sin_c, cos_p, sin_p, attn_ref)
    trig_ref[0] = cos_c
    trig_ref[1] = sin_c
    _mlstm_compute(qk_ref, v_ref, og_ref, gates_ref, cw_ref, gn_ref, mls_ref, hist_ref, c_ref, m_ref)


def _mixers(proj, side, pos_f, inv_lane, sinks, conv_w, g_norm, B, S, q_width, k_blk, v_blk,
            qk_blk, mv_blk, og_blk, gates_blk, qk_width, v_width):
    L = ATTN_BLOCK
    assert MLSTM_CHUNK == L
    nb = S // L
    dv = v_width // MLSTM_HEADS
    dqk = qk_width // (2 * MLSTM_HEADS)

    def cur(b, n, s):
        return b * nb + n

    def prev(b, n, s):
        return b * nb + jnp.maximum(n - 1, 0)

    grid_spec = pltpu.PrefetchScalarGridSpec(
        num_scalar_prefetch=1,
        grid=(B, nb),
        in_specs=[pl.BlockSpec((L, q_width), lambda b, n, s: (cur(b, n, s), 0)),
                  pl.BlockSpec((L, LANES), lambda b, n, s: (cur(b, n, s), k_blk)),
                  pl.BlockSpec((L, LANES), lambda b, n, s: (prev(b, n, s), k_blk)),
                  pl.BlockSpec((L, LANES), lambda b, n, s: (cur(b, n, s), v_blk)),
                  pl.BlockSpec((L, LANES), lambda b, n, s: (prev(b, n, s), v_blk)),
                  pl.BlockSpec((L, 1), lambda b, n, s: (cur(b, n, s), 0)),
                  pl.BlockSpec((1, LANES), lambda b, n, s: (0, 0)),
                  pl.BlockSpec((L, qk_width), lambda b, n, s: (cur(b, n, s), qk_blk)),
                  pl.BlockSpec((L, v_width), lambda b, n, s: (cur(b, n, s), mv_blk)),
                  pl.BlockSpec((L, v_width), lambda b, n, s: (cur(b, n, s), og_blk)),
                  pl.BlockSpec((L, LANES), lambda b, n, s: (cur(b, n, s), gates_blk)),
                  pl.BlockSpec((CONV_WIDTH, qk_width), lambda b, n, s: (0, 0)),
                  pl.BlockSpec((1, v_width), lambda b, n, s: (0, 0))],
        out_specs=[pl.BlockSpec((L, q_width), lambda b, n, s: (cur(b, n, s), 0)),
                   pl.BlockSpec((L, v_width), lambda b, n, s: (cur(b, n, s), 0))],
        scratch_shapes=[pltpu.VMEM((2, L, LANES), F32),
                        pltpu.VMEM((L + 2 * SUBLANES, qk_width), F32),
                        pltpu.VMEM((MLSTM_HEADS, dqk, dv + LANES), F32),
                        pltpu.VMEM((SUBLANES, LANES), F32)],
    )
    return pl.pallas_call(
        _mixers_body,
        grid_spec=grid_spec,
        out_shape=[jax.ShapeDtypeStruct((B * S, q_width), BF16),
                   jax.ShapeDtypeStruct((B * S, v_width), BF16)],
        compiler_params=_cparams("arbitrary", "arbitrary"),
        name="mixers",
    )(sinks, proj, side, side, side, side, pos_f, inv_lane, proj, proj, proj, side, conv_w, g_norm)


def _merge_body(a_ref, m_ref, ga_ref, gm_ref, wa_ref, wm_ref, o_ref):
    ya = jnp.dot(a_ref[...], wa_ref[...], preferred_element_type=F32)
    ym = jnp.dot(m_ref[...], wm_ref[...], preferred_element_type=F32)
    merged = _sigmoid(ga_ref[...].astype(F32)) * ya + _sigmoid(gm_ref[...].astype(F32)) * ym
    o_ref[...] = merged.astype(BF16)


def _merge(attn_o, mls_o, proj, wa, wm, ga_blk, gm_blk, tm=1024):
    N = attn_o.shape[0]
    D = wa.shape[1]
    wa_w, wm_w = attn_o.shape[1], mls_o.shape[1]
    const = lambda i: (0, 0)
    return pl.pallas_call(
        _merge_body,
        grid=(N // tm,),
        in_specs=[pl.BlockSpec((tm, wa_w), lambda i: (i, 0)),
                  pl.BlockSpec((tm, wm_w), lambda i: (i, 0)),
                  pl.BlockSpec((tm, D), lambda i: (i, ga_blk)),
                  pl.BlockSpec((tm, D), lambda i: (i, gm_blk)),
                  pl.BlockSpec((wa_w, D), const),
                  pl.BlockSpec((wm_w, D), const)],
        out_specs=pl.BlockSpec((tm, D), lambda i: (i, 0)),
        out_shape=jax.ShapeDtypeStruct((N, D), BF16),
        compiler_params=_cparams("arbitrary"),
        name="merge",
    )(attn_o, mls_o, proj, proj, wa, wm)


def _split_bf16(v):
    hi = v.astype(BF16)
    return hi, (v - hi.astype(F32)).astype(BF16)


def _outproj_body(mg_ref, x_ref, mod_ref, gpost_ref, gpre_ref, wo_ref, wr_ref, br_ref, x1_ref, h2_ref, lg_ref):
    y = jnp.dot(mg_ref[...], wo_ref[...], preferred_element_type=F32)
    x1 = x_ref[...] + mod_ref[0, 2:3, :] * _rms(y, gpost_ref[...])
    x1_ref[...] = x1
    h2 = _rms(x1, gpre_ref[...]) * (1.0 + mod_ref[0, 4:5, :]) + mod_ref[0, 3:4, :]
    h2_ref[...] = h2
    h_hi, h_lo = _split_bf16(h2)
    w_hi, w_lo = _split_bf16(wr_ref[...])
    lg = (jnp.dot(h_hi, w_hi, preferred_element_type=F32) + jnp.dot(h_hi, w_lo, preferred_element_type=F32)
          + jnp.dot(h_lo, w_hi, preferred_element_type=F32))
    lg_ref[...] = lg + br_ref[...]


def _out_proj(merged, x2, mod3, g_post, g_pre, wo, wr, br, rows_per_batch, tm=512):
    N, D = x2.shape
    tpb = rows_per_batch // tm
    const = lambda i: (0, 0)
    return pl.pallas_call(
        _outproj_body,
        grid=(N // tm,),
        in_specs=[pl.BlockSpec((tm, D), lambda i: (i, 0)),
                  pl.BlockSpec((tm, D), lambda i: (i, 0)),
                  pl.BlockSpec((1, 6, D), lambda i: (i // tpb, 0, 0)),
                  pl.BlockSpec((1, D), const),
                  pl.BlockSpec((1, D), const),
                  pl.BlockSpec((D, D), const),
                  pl.BlockSpec((D, LANES), const),
                  pl.BlockSpec((1, LANES), const)],
        out_specs=[pl.BlockSpec((tm, D), lambda i: (i, 0)),
                   pl.BlockSpec((tm, D), lambda i: (i, 0)),
                   pl.BlockSpec((tm, LANES), lambda i: (i, 0))],
        out_shape=[jax.ShapeDtypeStruct((N, D), F32),
                   jax.ShapeDtypeStruct((N, D), F32),
                   jax.ShapeDtypeStruct((N, LANES), F32)],
        compiler_params=_cparams("arbitrary"),
        name="out_proj",
    )(merged, x2, mod3, g_post, g_pre, wo, wr, br)


def _route_body(lg_ref, e_ref, w_ref, r_ref, cnt_ref, run_ref):
    T = lg_ref.shape[0]
    E = N_EXPERTS

    @pl.when(pl.program_id(0) == 0)
    def _():
        run_ref[...] = jnp.zeros(run_ref.shape, F32)

    lg = lg_ref[...].T[0:E, :]
    eidx = lax.broadcasted_iota(I32, (E, T), 0)
    tops, idxs, hots = [], [], []
    for _ in range(TOP_K):
        mx = jnp.max(lg, axis=0, keepdims=True)
        ik = jnp.min(jnp.where(lg == mx, eidx, E), axis=0, keepdims=True)
        hot = eidx == ik
        lg = jnp.where(hot, NEG_INF, lg)
        tops.append(mx)
        idxs.append(ik)
        hots.append(hot)
    exps = [jnp.exp(t - tops[0]) for t in tops]
    tot = exps[0] + exps[1] + exps[2] + exps[3]

    sel = jnp.zeros((E, T), F32)
    for hot in hots:
        sel = sel + jnp.where(hot, 1.0, 0.0)
    before = lax.broadcasted_iota(I32, (T, T), 0) < lax.broadcasted_iota(I32, (T, T), 1)
    excl = jnp.dot(sel.astype(BF16), jnp.where(before, 1.0, 0.0).astype(BF16), preferred_element_type=F32)
    base = excl + run_ref[:, 0:1]
    for k in range(TOP_K):
        e_ref[k:k + 1, :] = idxs[k]
        w_ref[k:k + 1, :] = exps[k] / tot
        r_ref[k:k + 1, :] = jnp.sum(jnp.where(hots[k], base, 0.0), axis=0, keepdims=True).astype(I32)
    run_ref[...] = run_ref[...] + jnp.sum(sel, axis=1, keepdims=True)
    cnt_ref[...] = run_ref[...]


def _route(logits, T=512):
    N = logits.shape[0]
    return pl.pallas_call(
        _route_body,
        grid=(N // T,),
        in_specs=[pl.BlockSpec((T, LANES), lambda i: (i, 0))],
        out_specs=[pl.BlockSpec((TOP_K, T), lambda i: (0, i)),
                   pl.BlockSpec((TOP_K, T), lambda i: (0, i)),
                   pl.BlockSpec((TOP_K, T), lambda i: (0, i)),
                   pl.BlockSpec((N_EXPERTS, LANES), lambda i: (0, 0))],
        out_shape=[jax.ShapeDtypeStruct((TOP_K, N), I32),
                   jax.ShapeDtypeStruct((TOP_K, N), F32),
                   jax.ShapeDtypeStruct((TOP_K, N), I32),
                   jax.ShapeDtypeStruct((N_EXPERTS, LANES), F32)],
        scratch_shapes=[pltpu.VMEM((N_EXPERTS, LANES), F32)],
        compiler_params=_cparams("arbitrary"),
        name="route",
    )(logits)


def _dest_body(e_ref, r_ref, ps_ref, d_ref):
    T = e_ref.shape[1]
    eidx = lax.broadcasted_iota(I32, (N_EXPERTS, T), 0)
    starts = ps_ref[:, 0:1]
    for k in range(TOP_K):
        base = jnp.sum(jnp.where(eidx == e_ref[k:k + 1, :], starts, 0.0), axis=0, keepdims=True)
        d_ref[k:k + 1, :] = r_ref[k:k + 1, :] + base.astype(I32)


def _dest_rows(top_e, rank, pad_starts_f, T=2048):
    N = top_e.shape[1]
    return pl.pallas_call(
        _dest_body,
        grid=(N // T,),
        in_specs=[pl.BlockSpec((TOP_K, T), lambda i: (0, i)),
                  pl.BlockSpec((TOP_K, T), lambda i: (0, i)),
                  pl.BlockSpec((N_EXPERTS, LANES), lambda i: (0, 0))],
        out_specs=pl.BlockSpec((TOP_K, T), lambda i: (0, i)),
        out_shape=jax.ShapeDtypeStruct((TOP_K, N), I32),
        compiler_params=_cparams("arbitrary"),
        name="dest_rows",
    )(top_e, rank, pad_starts_f)


def _dispatch_body(zf_ref, dest_ref, h_ref, xs_hbm, zeros_ref, sem, zsem):
    tc = h_ref.shape[0]
    tm = zeros_ref.shape[0]

    @pl.when(pl.program_id(0) == 0)
    def _():
        zeros_ref[...] = jnp.zeros(zeros_ref.shape, zeros_ref.dtype)

        def tile_copy(i):
            return pltpu.make_async_copy(zeros_ref, xs_hbm.at[pl.ds(pl.multiple_of(i * tm, tm), tm), :], zsem)

        def zero_issue(i, carry):
            @pl.when(zf_ref[i] != 0)
            def _():
                tile_copy(i).start()
            return carry

        def zero_drain(i, carry):
            @pl.when(zf_ref[i] != 0)
            def _():
                tile_copy(i).wait()
            return carry

        lax.fori_loop(0, zf_ref.shape[0], zero_issue, 0)
        lax.fori_loop(0, zf_ref.shape[0], zero_drain, 0)

    def row_copy(t, k):
        return pltpu.make_async_copy(h_ref.at[pl.ds(t, 1), :], xs_hbm.at[pl.ds(dest_ref[k, t], 1), :], sem)

    def issue(t, carry):
        for k in range(TOP_K):
            row_copy(t, k).start()
        return carry

    def drain(t, carry):
        for k in range(TOP_K):
            row_copy(t, k).wait()
        return carry

    lax.fori_loop(0, tc, issue, 0)
    lax.fori_loop(0, tc, drain, 0)


def _dispatch(zero_tiles, dest, h2, n_rows, tm, tc=512):
    N, D = h2.shape
    grid_spec = pltpu.PrefetchScalarGridSpec(
        num_scalar_prefetch=1,
        grid=(N // tc,),
        in_specs=[pl.BlockSpec((TOP_K, tc), lambda i, zf: (0, i), memory_space=pltpu.SMEM),
                  pl.BlockSpec((tc, D), lambda i, zf: (i, 0))],
        out_specs=pl.BlockSpec(memory_space=pl.ANY),
        scratch_shapes=[pltpu.VMEM((tm, D), h2.dtype),
                        pltpu.SemaphoreType.DMA(()),
                        pltpu.SemaphoreType.DMA(())],
    )
    return pl.pallas_call(
        _dispatch_body,
        grid_spec=grid_spec,
        out_shape=jax.ShapeDtypeStruct((n_rows, D), h2.dtype),
        compiler_params=_cparams("arbitrary"),
        name="dispatch",
    )(zero_tiles, dest, h2)


PAIR_CHUNK = 2 * LANES
UP_CHUNK = 1024


def _new_weights(te_ref, i):
    return (i == 0) | (te_ref[i] != te_ref[jnp.maximum(i - 1, 0)])


def _ffn_up_body(te_ref, nv_ref, x_ref, w_ref, b_ref, sel_ref, o_ref, wb_ref):
    i = pl.program_id(1)

    @pl.when((i < nv_ref[0]) & _new_weights(te_ref, i))
    def _():
        wb_ref[...] = w_ref[0].astype(BF16)

    @pl.when(i < nv_ref[0])
    def _():
        xb = x_ref[...].astype(BF16)
        for c in range(wb_ref.shape[1] // UP_CHUNK):
            cols = slice(c * UP_CHUNK, (c + 1) * UP_CHUNK)
            gu = jnp.dot(xb, wb_ref[:, cols], preferred_element_type=F32) + b_ref[0, :, cols]
            glu = jnp.minimum(gu, SWIGLU_LIMIT)
            half = 0.5 * glu
            gate = half + half * jnp.tanh(SWIGLU_ALPHA * half)
            lin = jnp.clip(gu, -SWIGLU_LIMIT, SWIGLU_LIMIT) + 1.0
            for p in range(UP_CHUNK // PAIR_CHUNK):
                parts = []
                for v in range(PAIR_CHUNK // LANES):
                    lo = p * PAIR_CHUNK + v * LANES
                    parts.append(gate[:, lo:lo + LANES] * pltpu.roll(lin[:, lo:lo + LANES], LANES - 1, 1))
                prod = jnp.concatenate(parts, axis=1).astype(BF16)
                out_lo = (c * UP_CHUNK + p * PAIR_CHUNK) // 2
                o_ref[:, out_lo:out_lo + LANES] = jnp.dot(prod, sel_ref[...], preferred_element_type=F32).astype(BF16)

    @pl.when(i >= nv_ref[0])
    def _():
        o_ref[...] = jnp.zeros(o_ref.shape, o_ref.dtype)


def _ffn_up(tile_e, n_valid, xs, w_gate_up, b_gate_up, tm, tf=1024):
    P, D = xs.shape
    E, _, F2 = w_gate_up.shape
    F = F2 // 2
    sel = (jnp.arange(PAIR_CHUNK)[:, None] == 2 * jnp.arange(LANES)[None, :]).astype(BF16)

    def row(i, nv):
        return jnp.minimum(i, nv[0] - 1)

    grid_spec = pltpu.PrefetchScalarGridSpec(
        num_scalar_prefetch=2,
        grid=(F // tf, P // tm),
        in_specs=[pl.BlockSpec((tm, D), lambda j, i, te, nv: (row(i, nv), 0)),
                  pl.BlockSpec((1, D, 2 * tf), lambda j, i, te, nv: (te[row(i, nv)], 0, j)),
                  pl.BlockSpec((1, 1, 2 * tf), lambda j, i, te, nv: (te[row(i, nv)], 0, j)),
                  pl.BlockSpec((PAIR_CHUNK, LANES), lambda j, i, te, nv: (0, 0))],
        out_specs=pl.BlockSpec((tm, tf), lambda j, i, te, nv: (i, j)),
        scratch_shapes=[pltpu.VMEM((D, 2 * tf), BF16)],
    )
    return pl.pallas_call(
        _ffn_up_body,
        grid_spec=grid_spec,
        out_shape=jax.ShapeDtypeStruct((P, F), BF16),
        compiler_params=_cparams("arbitrary", "arbitrary"),
        name="ffn_up",
    )(tile_e, n_valid, xs, w_gate_up, b_gate_up.reshape(E, 1, F2), sel)


def _ffn_down_body(te_ref, nv_ref, a_ref, w_ref, b_ref, o_ref, wb_ref):
    i = pl.program_id(1)

    @pl.when((i < nv_ref[0]) & _new_weights(te_ref, i))
    def _():
        wb_ref[...] = w_ref[0].astype(BF16)

    @pl.when(i < nv_ref[0])
    def _():
        o_ref[...] = jnp.dot(a_ref[...], wb_ref[...], preferred_element_type=F32) + b_ref[0]

    @pl.when(i >= nv_ref[0])
    def _():
        o_ref[...] = jnp.zeros(o_ref.shape, o_ref.dtype)


def _ffn_down(tile_e, n_valid, act, w_down, b_down, tm, tn=2048):
    P, F = act.shape
    E, _, D = w_down.shape

    def row(i, nv):
        return jnp.minimum(i, nv[0] - 1)

    grid_spec = pltpu.PrefetchScalarGridSpec(
        num_scalar_prefetch=2,
        grid=(D // tn, P // tm),
        in_specs=[pl.BlockSpec((tm, F), lambda j, i, te, nv: (row(i, nv), 0)),
                  pl.BlockSpec((1, F, tn), lambda j, i, te, nv: (te[row(i, nv)], 0, j)),
                  pl.BlockSpec((1, 1, tn), lambda j, i, te, nv: (te[row(i, nv)], 0, j))],
        out_specs=pl.BlockSpec((tm, tn), lambda j, i, te, nv: (i, j)),
        scratch_shapes=[pltpu.VMEM((F, tn), BF16)],
    )
    return pl.pallas_call(
        _ffn_down_body,
        grid_spec=grid_spec,
        out_shape=jax.ShapeDtypeStruct((P, D), F32),
        compiler_params=_cparams("arbitrary", "arbitrary"),
        name="ffn_down",
    )(tile_e, n_valid, act, w_down, b_down.reshape(E, 1, D))


def _combine_body(dcur_ref, dnxt_ref, w_ref, x1_ref, mod_ref, g_ref, y_hbm, o_ref, buf_ref, sem_ref):
    i = pl.program_id(0)
    n = pl.num_programs(0)
    tt = x1_ref.shape[0]
    slot = i % 2

    def row_copy(d_ref, s, t, k):
        return pltpu.make_async_copy(y_hbm.at[pl.ds(d_ref[k, t], 1), :],
                                     buf_ref.at[s, k, pl.ds(t, 1), :], sem_ref.at[s])

    def issue(d_ref, s):
        def body(t, carry):
            for k in range(TOP_K):
                row_copy(d_ref, s, t, k).start()
            return carry
        lax.fori_loop(0, tt, body, 0)

    @pl.when(i == 0)
    def _():
        issue(dcur_ref, 0)

    @pl.when(i + 1 < n)
    def _():
        issue(dnxt_ref, 1 - slot)

    def drain(t, carry):
        for k in range(TOP_K):
            row_copy(dcur_ref, slot, t, k).wait()
        return carry
    lax.fori_loop(0, tt, drain, 0)

    acc = jnp.zeros(x1_ref.shape, F32)
    for k in range(TOP_K):
        acc = acc + w_ref[:, k:k + 1] * buf_ref[slot, k]
    o_ref[...] = x1_ref[...] + mod_ref[0, 5:6, :] * _rms(acc, g_ref[...])


def _combine(dest, top_w_t, x1, mod3, g_post, y_rows, rows_per_batch, tt=256):
    N, D = x1.shape
    nt = N // tt
    tpb = rows_per_batch // tt
    return pl.pallas_call(
        _combine_body,
        grid=(nt,),
        in_specs=[pl.BlockSpec((TOP_K, tt), lambda i: (0, i), memory_space=pltpu.SMEM),
                  pl.BlockSpec((TOP_K, tt), lambda i: (0, jnp.minimum(i + 1, nt - 1)), memory_space=pltpu.SMEM),
                  pl.BlockSpec((tt, TOP_K), lambda i: (i, 0)),
                  pl.BlockSpec((tt, D), lambda i: (i, 0)),
                  pl.BlockSpec((1, 6, D), lambda i: (i // tpb, 0, 0)),
                  pl.BlockSpec((1, D), lambda i: (0, 0)),
                  pl.BlockSpec(memory_space=pl.ANY)],
        out_specs=pl.BlockSpec((tt, D), lambda i: (i, 0)),
        out_shape=jax.ShapeDtypeStruct((N, D), F32),
        scratch_shapes=[pltpu.VMEM((2, TOP_K, tt, D), F32),
                        pltpu.SemaphoreType.DMA((2,))],
        compiler_params=_cparams("arbitrary"),
        name="combine",
    )(dest, dest, top_w_t, x1, mod3, g_post, y_rows)


def _block(x, mod3, positions, g_pre_mix, g_post_mix, g_pre_ffn, g_post_ffn, w_in, b_in, conv_qk, sinks, g_mlstm,
           w_attn_proj, w_mlstm_proj, w_out, w_router, b_router, w_gate_up, b_gate_up, w_down, b_down, ffn_tm=512):
    B, S, D = x.shape
    N = B * S
    aq_w = w_attn_proj.shape[0]
    akv_w = (aq_w // ATTN_HEAD_DIM // ATTN_GROUP) * ATTN_HEAD_DIM
    mv_w = w_mlstm_proj.shape[0]
    mqk_w = mv_w // 2
    H = MLSTM_HEADS
    sizes = (aq_w, akv_w, akv_w, mqk_w, mqk_w, mv_w, mv_w, H, H, D, D)
    offs = [0]
    for sz in sizes:
        offs.append(offs[-1] + sz)
    seg_w = lambda k: w_in[:, offs[k]:offs[k + 1]]
    seg_b = lambda k: b_in[offs[k]:offs[k + 1]]
    order = (0, 3, 4, 5, 6, 9, 10)
    w_main = jnp.concatenate([seg_w(k) for k in order], axis=1).astype(BF16)
    b_main = jnp.concatenate([seg_b(k) for k in order]).reshape(1, -1)
    pad = LANES - 2 * H
    w_side = jnp.concatenate([seg_w(1), seg_w(2), seg_w(7), seg_w(8), jnp.zeros((D, pad), F32)], axis=1).astype(BF16)
    b_side = jnp.concatenate([seg_b(1), seg_b(2), seg_b(7), seg_b(8), jnp.zeros((pad,), F32)]).reshape(1, -1)

    x2 = x.reshape(N, D)
    proj, side = _in_proj(x2, mod3, g_pre_mix.reshape(1, D), w_main, b_main, w_side, b_side, S)

    c_mq = aq_w
    c_mv = aq_w + 2 * mqk_w
    c_mo = c_mv + mv_w
    c_ga = c_mo + mv_w
    c_gm = c_ga + D
    side_k, side_v, side_gates = 0, akv_w // LANES, 2 * akv_w // LANES

    half = ROPE_DIM // 2
    inv_freq = ROPE_THETA ** (-jnp.arange(half, dtype=F32) / half)
    lane_d = jnp.arange(LANES) % ATTN_HEAD_DIM
    inv_lane = jnp.where(lane_d < ROPE_DIM, inv_freq[lane_d % half], 0.0).reshape(1, LANES).astype(F32)
    pos_f = positions.astype(F32).reshape(N, 1)
    attn_o, mls_o = _mixers(proj, side, pos_f, inv_lane, sinks, conv_qk, g_mlstm.reshape(1, mv_w), B, S,
                            aq_w, side_k, side_v, c_mq // (2 * mqk_w), c_mv // mv_w, c_mo // mv_w, side_gates,
                            2 * mqk_w, mv_w)

    E = N_EXPERTS
    wr = jnp.concatenate([w_router, jnp.zeros((D, LANES - E), F32)], axis=1)
    br = jnp.concatenate([b_router, jnp.full((LANES - E,), -1e30, F32)]).reshape(1, LANES)
    merged = _merge(attn_o, mls_o, proj, w_attn_proj.astype(BF16), w_mlstm_proj.astype(BF16), c_ga // D, c_gm // D)
    x1, h2, logits = _out_proj(merged, x2, mod3, g_post_mix.reshape(1, D), g_pre_ffn.reshape(1, D),
                               w_out.astype(BF16), wr, br, S)

    top_e, top_w, rank, counts = _route(logits)
    cnt = counts[:, 0].astype(I32)
    padded = (cnt + ffn_tm - 1) // ffn_tm * ffn_tm
    pad_ends = jnp.cumsum(padded)
    pad_starts = pad_ends - padded
    n_rows = N * TOP_K + E * ffn_tm
    n_tiles = n_rows // ffn_tm
    tile_start = jnp.arange(n_tiles, dtype=I32) * ffn_tm
    tile_e = jnp.minimum(jnp.sum((pad_ends[None, :] <= tile_start[:, None]).astype(I32), axis=1), E - 1)
    n_valid = (pad_ends[E - 1:E] // ffn_tm).astype(I32)
    dest = _dest_rows(top_e, rank, jnp.broadcast_to(pad_starts.astype(F32)[:, None], (E, LANES)))

    tile_end = tile_start + ffn_tm
    zero_tiles = (jnp.any(pad_ends[None, :] == tile_end[:, None], axis=1)
                  | (tile_start >= pad_ends[E - 1])).astype(I32)
    xs = _dispatch(zero_tiles, dest, h2, n_rows, ffn_tm)
    act = _ffn_up(tile_e, n_valid, xs, w_gate_up, b_gate_up, ffn_tm)
    y_rows = _ffn_down(tile_e, n_valid, act, w_down, b_down, ffn_tm)
    out = _combine(dest, top_w.T, x1, mod3, g_post_ffn.reshape(1, D), y_rows, S)
    return out.reshape(B, S, D)


def kernel(x, c, positions, w_ada, b_ada, g_pre_mix, g_post_mix, g_pre_ffn, g_post_ffn, w_in, b_in, conv_qk, sinks,
           g_mlstm, w_attn_proj, w_mlstm_proj, w_out, w_router, b_router, w_gate_up, b_gate_up, w_down, b_down):
    B, S, D = x.shape
    depth = w_ada.shape[0]
    for l in range(depth):
        mod3 = _ada_ln(c, w_ada[l], b_ada[l]).reshape(B, 6, D)
        x = _block(x, mod3, positions, g_pre_mix[l], g_post_mix[l], g_pre_ffn[l], g_post_ffn[l], w_in[l], b_in[l],
                   conv_qk[l], sinks[l], g_mlstm[l], w_attn_proj[l], w_mlstm_proj[l], w_out[l], w_router[l],
                   b_router[l], w_gate_up[l], b_gate_up[l], w_down[l], b_down[l])
    return x
```

```python
import jax
import jax.numpy as jnp
from jax import lax
from jax.experimental import pallas as pl
from jax.experimental.pallas import tpu as pltpu

F32 = jnp.float32
BF16 = jnp.bfloat16
I32 = jnp.int32
HIGHEST = lax.Precision.HIGHEST

ATTN_HEAD_DIM = 64
ATTN_GROUP = 8
WINDOW = 128
ATTN_BLOCK = 128
ROPE_THETA = 500000.0
ROPE_DIM = ATTN_HEAD_DIM // 4
MLSTM_HEADS = 4
MLSTM_CHUNK = 128
CONV_WIDTH = 4
GATE_SOFTCAP = 15.0
N_EXPERTS = 32
TOP_K = 4
SWIGLU_ALPHA = 1.702
SWIGLU_LIMIT = 7.0
NORM_EPS = 1e-6

LANES = 128
SUBLANES = 8
VMEM_LIMIT = 60 * 1024 * 1024

NEG_INF = float("-inf")


def _cparams(*sem):
    return pltpu.CompilerParams(dimension_semantics=sem, vmem_limit_bytes=VMEM_LIMIT)


def _rms(x, g):
    ms = jnp.mean(x * x, axis=-1, keepdims=True)
    return x * lax.rsqrt(ms + NORM_EPS) * g


def _sigmoid(x):
    return 1.0 / (1.0 + jnp.exp(-x))


def _ada_body(c_ref, w_ref, b_ref, o_ref):
    c = c_ref[...]
    cond = c * _sigmoid(c)
    o_ref[...] = jnp.dot(cond, w_ref[...], preferred_element_type=F32, precision=HIGHEST) + b_ref[...]


def _ada_ln(c, w_ada, b_ada):
    B, D = c.shape
    W = w_ada.shape[1]
    tn = 1024
    cp = jnp.zeros((SUBLANES, D), F32).at[:B].set(c)
    out = pl.pallas_call(
        _ada_body,
        grid=(W // tn,),
        in_specs=[pl.BlockSpec((SUBLANES, D), lambda j: (0, 0)),
                  pl.BlockSpec((D, tn), lambda j: (0, j)),
                  pl.BlockSpec((1, tn), lambda j: (0, j))],
        out_specs=pl.BlockSpec((SUBLANES, tn), lambda j: (0, j)),
        out_shape=jax.ShapeDtypeStruct((SUBLANES, W), F32),
        compiler_params=_cparams("arbitrary"),
        name="ada_ln",
    )(cp, w_ada, b_ada.reshape(1, W))
    return out[:B]


def _inproj_body(x_ref, mod_ref, g_ref, w_ref, b_ref, wg_ref, bg_ref, o_ref, og_ref, h_ref):
    @pl.when(pl.program_id(1) == 0)
    def _():
        h = _rms(x_ref[...], g_ref[...]) * (1.0 + mod_ref[0, 1:2, :]) + mod_ref[0, 0:1, :]
        hb = h.astype(BF16)
        h_ref[...] = hb
        og_ref[...] = jnp.dot(hb, wg_ref[...], preferred_element_type=F32) + bg_ref[...]

    acc = jnp.dot(h_ref[...], w_ref[...], preferred_element_type=F32) + b_ref[...]
    o_ref[...] = acc.astype(BF16)


def _in_proj(x2, mod3, g, w_main, b_main, w_side, b_side, rows_per_batch, tm=1024, tn=1024):
    N, D = x2.shape
    W = w_main.shape[1]
    WS = w_side.shape[1]
    tpb = rows_per_batch // tm
    return pl.pallas_call(
        _inproj_body,
        grid=(N // tm, W // tn),
        in_specs=[pl.BlockSpec((tm, D), lambda i, j: (i, 0)),
                  pl.BlockSpec((1, 6, D), lambda i, j: (i // tpb, 0, 0)),
                  pl.BlockSpec((1, D), lambda i, j: (0, 0)),
                  pl.BlockSpec((D, tn), lambda i, j: (0, j)),
                  pl.BlockSpec((1, tn), lambda i, j: (0, j)),
                  pl.BlockSpec((D, WS), lambda i, j: (0, 0)),
                  pl.BlockSpec((1, WS), lambda i, j: (0, 0))],
        out_specs=[pl.BlockSpec((tm, tn), lambda i, j: (i, j)),
                   pl.BlockSpec((tm, WS), lambda i, j: (i, 0))],
        out_shape=[jax.ShapeDtypeStruct((N, W), BF16),
                   jax.ShapeDtypeStruct((N, WS), F32)],
        scratch_shapes=[pltpu.VMEM((tm, D), BF16)],
        compiler_params=_cparams("arbitrary", "arbitrary"),
        name="in_proj",
    )(x2, mod3, g, w_main, b_main, w_side, b_side)


def _attn_compute(n, sinks_ref, q_ref, kc_ref, kp_ref, vc_ref, vp_ref, cos_c, sin_c, cos_p, sin_p, o_ref):
    L = ATTN_BLOCK
    lane = lax.broadcasted_iota(I32, (1, LANES), 1)
    first_half = (lane % ATTN_HEAD_DIM) < (ROPE_DIM // 2)
    low_lanes = lane < ATTN_HEAD_DIM

    def rope(x, cs, sn):
        swapped = jnp.where(first_half, -pltpu.roll(x, LANES - ROPE_DIM // 2, 1), pltpu.roll(x, ROPE_DIM // 2, 1))
        return x * cs + swapped * sn

    k2 = jnp.concatenate([rope(kp_ref[...].astype(F32), cos_p, sin_p),
                          rope(kc_ref[...].astype(F32), cos_c, sin_c)], axis=0)
    v2 = jnp.concatenate([vp_ref[...], vc_ref[...]], axis=0).astype(F32)
    k2s = pltpu.roll(k2, ATTN_HEAD_DIM, 1)
    v2s = pltpu.roll(v2, ATTN_HEAD_DIM, 1)
    zero = jnp.zeros_like(k2)
    k_lo = [jnp.where(low_lanes, k2, zero).astype(BF16), jnp.where(low_lanes, k2s, zero).astype(BF16)]
    k_hi = [jnp.where(low_lanes, zero, k2s).astype(BF16), jnp.where(low_lanes, zero, k2).astype(BF16)]
    v_lo = [jnp.where(low_lanes, v2, zero).astype(BF16), jnp.where(low_lanes, v2s, zero).astype(BF16)]
    v_hi = [jnp.where(low_lanes, zero, v2s).astype(BF16), jnp.where(low_lanes, zero, v2).astype(BF16)]

    qi = lax.broadcasted_iota(I32, (L, 2 * L), 0)
    kj = lax.broadcasted_iota(I32, (L, 2 * L), 1)
    diff = qi + L - kj
    valid = (diff >= 0) & (diff < WINDOW) & ((kj >= L) | (n > 0))

    scale = ATTN_HEAD_DIM ** -0.5
    n_pairs = q_ref.shape[1] // LANES
    for j in range(n_pairs):
        hk = (2 * j) // ATTN_GROUP
        qp = (rope(q_ref[:, j * LANES:(j + 1) * LANES].astype(F32), cos_c, sin_c) * scale).astype(BF16)
        acc = jnp.zeros((L, LANES), F32)
        for half, (kx, vx) in enumerate(((k_lo[hk], v_lo[hk]), (k_hi[hk], v_hi[hk]))):
            s = lax.dot_general(qp, kx, (((1,), (1,)), ((), ())), preferred_element_type=F32)
            s = jnp.where(valid, s, NEG_INF)
            sink = sinks_ref[2 * j + half]
            m = jnp.maximum(jnp.max(s, axis=-1, keepdims=True), sink)
            p = jnp.exp(s - m)
            denom = jnp.sum(p, axis=-1, keepdims=True) + jnp.exp(sink - m)
            acc = acc + jnp.dot(p.astype(BF16), vx, preferred_element_type=F32) / denom
        o_ref[:, j * LANES:(j + 1) * LANES] = acc.astype(BF16)


def _log_sigmoid(z):
    return jnp.minimum(z, 0.0) - jnp.log1p(jnp.exp(-jnp.abs(z)))


def _mlstm_compute(qk_ref, v_ref, og_ref, gates_ref, cw_ref, gn_ref, out_ref, hist_ref, c_ref, m_ref):
    L = MLSTM_CHUNK
    H = MLSTM_HEADS
    dqk = qk_ref.shape[1] // (2 * H)
    dv = v_ref.shape[1] // H
    tail = SUBLANES

    hist_ref[tail:tail + L, :] = qk_ref[...].astype(F32)
    conv = hist_ref[tail:tail + L, :] * cw_ref[CONV_WIDTH - 1:CONV_WIDTH, :]
    for t in range(CONV_WIDTH - 1):
        off = tail - (CONV_WIDTH - 1) + t
        conv = conv + hist_ref[off:off + L, :] * cw_ref[t:t + 1, :]
    hist_ref[0:tail, :] = hist_ref[L:L + tail, :]
    qk = conv * _sigmoid(conv)

    gates = gates_ref[...]
    capped = GATE_SOFTCAP * jnp.tanh(gates / GATE_SOFTCAP)
    log_f = _log_sigmoid(capped)
    row = lax.broadcasted_iota(I32, (L, L), 0)
    col = lax.broadcasted_iota(I32, (L, L), 1)
    causal = col <= row
    b_all = jnp.dot(causal.astype(F32), log_f, preferred_element_type=F32, precision=HIGHEST)
    b_all_t = b_all.T
    capped_t = capped.T
    ones_col = jnp.where(lax.broadcasted_iota(I32, (L, LANES), 1) == 0, 1.0, 0.0).astype(BF16)

    for h in range(H):
        b_col = b_all[:, H + h:H + h + 1]
        li_col = capped[:, h:h + 1]
        b_row = b_all_t[H + h:H + h + 1, :]
        li_row = capped_t[h:h + 1, :]
        g_last = b_all[L - 1:L, H + h:H + h + 1]
        m_prev = m_ref[h:h + 1, 0:1]

        d_log = jnp.where(causal, b_col - b_row + li_row, NEG_INF)
        inter = b_col + m_prev
        m = jnp.maximum(inter, jnp.max(d_log, axis=-1, keepdims=True))
        q_h = qk[:, h * dqk:(h + 1) * dqk].astype(BF16)
        k_f = qk[:, (H + h) * dqk:(H + h + 1) * dqk] * (dqk ** -0.5)
        s_qk = lax.dot_general(q_h, k_f.astype(BF16), (((1,), (1,)), ((), ())),
                               preferred_element_type=F32) * jnp.exp(d_log - m)
        v_ext = jnp.concatenate([v_ref[:, h * dv:(h + 1) * dv], ones_col], axis=1)
        c_prev = c_ref[h]
        tot = (jnp.dot(s_qk.astype(BF16), v_ext, preferred_element_type=F32)
               + jnp.exp(inter - m) * jnp.dot(q_h, c_prev.astype(BF16), preferred_element_type=F32))
        h_t = tot[:, :dv] / jnp.maximum(jnp.abs(tot[:, dv:dv + 1]), jnp.exp(-m))

        a_col = g_last - b_col + li_col
        m_new = jnp.maximum(g_last + m_prev, jnp.max(a_col, axis=0, keepdims=True))
        wk = jnp.exp(a_col - m_new) * k_f
        kv = jnp.dot(wk.T.astype(BF16), v_ext, preferred_element_type=F32)
        c_ref[h] = jnp.exp(g_last + m_prev - m_new) * c_prev + kv
        m_ref[h:h + 1, :] = jnp.broadcast_to(m_new, (1, LANES))

        hn = _rms(h_t, gn_ref[:, h * dv:(h + 1) * dv])
        out_ref[:, h * dv:(h + 1) * dv] = (_sigmoid(og_ref[:, h * dv:(h + 1) * dv].astype(F32)) * hn).astype(BF16)


def _mixers_body(sinks_ref, q_ref, kc_ref, kp_ref, vc_ref, vp_ref, pos_ref, inv_ref,
                 qk_ref, v_ref, og_ref, gates_ref, cw_ref, gn_ref,
                 attn_ref, mls_ref, trig_ref, hist_ref, c_ref, m_ref):
    n = pl.program_id(1)
    ang = pos_ref[...] * inv_ref[...]
    cos_c, sin_c = jnp.cos(ang), jnp.sin(ang)

    @pl.when(n == 0)
    def _():
        trig_ref[0] = cos_c
        trig_ref[1] = sin_c
        hist_ref[0:SUBLANES, :] = jnp.zeros((SUBLANES, hist_ref.shape[1]), F32)
        c_ref[...] = jnp.zeros(c_ref.shape, F32)
        m_ref[...] = jnp.zeros(m_ref.shape, F32)

    cos_p, sin_p = trig_ref[0], trig_ref[1]
    _attn_compute(n, sinks_ref, q_ref, kc_ref, kp_ref, vc_ref, vp_ref, cos_c, sin_c, cos_p, sin_p, attn_ref)
    trig_ref[0] = cos_c
    trig_ref[1] = sin_c
    _mlstm_compute(qk_ref, v_ref, og_ref, gates_ref, cw_ref, gn_ref, mls_ref, hist_ref, c_ref, m_ref)


def _mixers(proj, side, pos_f, inv_lane, sinks, conv_w, g_norm, B, S, q_width, k_blk, v_blk,
            qk_blk, mv_blk, og_blk, gates_blk, qk_width, v_width):
    L = ATTN_BLOCK
    assert MLSTM_CHUNK == L
    nb = S // L
    dv = v_width // MLSTM_HEADS
    dqk = qk_width // (2 * MLSTM_HEADS)

    def cur(b, n, s):
        return b * nb + n

    def prev(b, n, s):
        return b * nb + jnp.maximum(n - 1, 0)

    grid_spec = pltpu.PrefetchScalarGridSpec(
        num_scalar_prefetch=1,
        grid=(B, nb),
        in_specs=[pl.BlockSpec((L, q_width), lambda b, n, s: (cur(b, n, s), 0)),
                  pl.BlockSpec((L, LANES), lambda b, n, s: (cur(b, n, s), k_blk)),
                  pl.BlockSpec((L, LANES), lambda b, n, s: (prev(b, n, s), k_blk)),
                  pl.BlockSpec((L, LANES), lambda b, n, s: (cur(b, n, s), v_blk)),
                  pl.BlockSpec((L, LANES), lambda b, n, s: (prev(b, n, s), v_blk)),
                  pl.BlockSpec((L, 1), lambda b, n, s: (cur(b, n, s), 0)),
                  pl.BlockSpec((1, LANES), lambda b, n, s: (0, 0)),
                  pl.BlockSpec((L, qk_width), lambda b, n, s: (cur(b, n, s), qk_blk)),
                  pl.BlockSpec((L, v_width), lambda b, n, s: (cur(b, n, s), mv_blk)),
                  pl.BlockSpec((L, v_width), lambda b, n, s: (cur(b, n, s), og_blk)),
                  pl.BlockSpec((L, LANES), lambda b, n, s: (cur(b, n, s), gates_blk)),
                  pl.BlockSpec((CONV_WIDTH, qk_width), lambda b, n, s: (0, 0)),
                  pl.BlockSpec((1, v_width), lambda b, n, s: (0, 0))],
        out_specs=[pl.BlockSpec((L, q_width), lambda b, n, s: (cur(b, n, s), 0)),
                   pl.BlockSpec((L, v_width), lambda b, n, s: (cur(b, n, s), 0))],
        scratch_shapes=[pltpu.VMEM((2, L, LANES), F32),
                        pltpu.VMEM((L + 2 * SUBLANES, qk_width), F32),
                        pltpu.VMEM((MLSTM_HEADS, dqk, dv + LANES), F32),
                        pltpu.VMEM((SUBLANES, LANES), F32)],
    )
    return pl.pallas_call(
        _mixers_body,
        grid_spec=grid_spec,
        out_shape=[jax.ShapeDtypeStruct((B * S, q_width), BF16),
                   jax.ShapeDtypeStruct((B * S, v_width), BF16)],
        compiler_params=_cparams("arbitrary", "arbitrary"),
        name="mixers",
    )(sinks, proj, side, side, side, side, pos_f, inv_lane, proj, proj, proj, side, conv_w, g_norm)


def _merge_body(a_ref, m_ref, ga_ref, gm_ref, wa_ref, wm_ref, o_ref):
    ya = jnp.dot(a_ref[...], wa_ref[...], preferred_element_type=F32)
    ym = jnp.dot(m_ref[...], wm_ref[...], preferred_element_type=F32)
    merged = _sigmoid(ga_ref[...].astype(F32)) * ya + _sigmoid(gm_ref[...].astype(F32)) * ym
    o_ref[...] = merged.astype(BF16)


def _merge(attn_o, mls_o, proj, wa, wm, ga_blk, gm_blk, tm=1024):
    N = attn_o.shape[0]
    D = wa.shape[1]
    wa_w, wm_w = attn_o.shape[1], mls_o.shape[1]
    const = lambda i: (0, 0)
    return pl.pallas_call(
        _merge_body,
        grid=(N // tm,),
        in_specs=[pl.BlockSpec((tm, wa_w), lambda i: (i, 0)),
                  pl.BlockSpec((tm, wm_w), lambda i: (i, 0)),
                  pl.BlockSpec((tm, D), lambda i: (i, ga_blk)),
                  pl.BlockSpec((tm, D), lambda i: (i, gm_blk)),
                  pl.BlockSpec((wa_w, D), const),
                  pl.BlockSpec((wm_w, D), const)],
        out_specs=pl.BlockSpec((tm, D), lambda i: (i, 0)),
        out_shape=jax.ShapeDtypeStruct((N, D), BF16),
        compiler_params=_cparams("arbitrary"),
        name="merge",
    )(attn_o, mls_o, proj, proj, wa, wm)


def _split_bf16(v):
    hi = v.astype(BF16)
    return hi, (v - hi.astype(F32)).astype(BF16)


def _outproj_body(mg_ref, x_ref, mod_ref, gpost_ref, gpre_ref, wo_ref, wr_ref, br_ref, x1_ref, h2_ref, lg_ref):
    y = jnp.dot(mg_ref[...], wo_ref[...], preferred_element_type=F32)
    x1 = x_ref[...] + mod_ref[0, 2:3, :] * _rms(y, gpost_ref[...])
    x1_ref[...] = x1
    h2 = _rms(x1, gpre_ref[...]) * (1.0 + mod_ref[0, 4:5, :]) + mod_ref[0, 3:4, :]
    h2_ref[...] = h2
    h_hi, h_lo = _split_bf16(h2)
    w_hi, w_lo = _split_bf16(wr_ref[...])
    lg = (jnp.dot(h_hi, w_hi, preferred_element_type=F32) + jnp.dot(h_hi, w_lo, preferred_element_type=F32)
          + jnp.dot(h_lo, w_hi, preferred_element_type=F32))
    lg_ref[...] = lg + br_ref[...]


def _out_proj(merged, x2, mod3, g_post, g_pre, wo, wr, br, rows_per_batch, tm=512):
    N, D = x2.shape
    tpb = rows_per_batch // tm
    const = lambda i: (0, 0)
    return pl.pallas_call(
        _outproj_body,
        grid=(N // tm,),
        in_specs=[pl.BlockSpec((tm, D), lambda i: (i, 0)),
                  pl.BlockSpec((tm, D), lambda i: (i, 0)),
                  pl.BlockSpec((1, 6, D), lambda i: (i // tpb, 0, 0)),
                  pl.BlockSpec((1, D), const),
                  pl.BlockSpec((1, D), const),
                  pl.BlockSpec((D, D), const),
                  pl.BlockSpec((D, LANES), const),
                  pl.BlockSpec((1, LANES), const)],
        out_specs=[pl.BlockSpec((tm, D), lambda i: (i, 0)),
                   pl.BlockSpec((tm, D), lambda i: (i, 0)),
                   pl.BlockSpec((tm, LANES), lambda i: (i, 0))],
        out_shape=[jax.ShapeDtypeStruct((N, D), F32),
                   jax.ShapeDtypeStruct((N, D), F32),
                   jax.ShapeDtypeStruct((N, LANES), F32)],
        compiler_params=_cparams("arbitrary"),
        name="out_proj",
    )(merged, x2, mod3, g_post, g_pre, wo, wr, br)


def _route_body(lg_ref, e_ref, w_ref, r_ref, cnt_ref, run_ref):
    T = lg_ref.shape[0]
    E = N_EXPERTS

    @pl.when(pl.program_id(0) == 0)
    def _():
        run_ref[...] = jnp.zeros(run_ref.shape, F32)

    lg = lg_ref[...].T[0:E, :]
    eidx = lax.broadcasted_iota(I32, (E, T), 0)
    tops, idxs, hots = [], [], []
    for _ in range(TOP_K):
        mx = jnp.max(lg, axis=0, keepdims=True)
        ik = jnp.min(jnp.where(lg == mx, eidx, E), axis=0, keepdims=True)
        hot = eidx == ik
        lg = jnp.where(hot, NEG_INF, lg)
        tops.append(mx)
        idxs.append(ik)
        hots.append(hot)
    exps = [jnp.exp(t - tops[0]) for t in tops]
    tot = exps[0] + exps[1] + exps[2] + exps[3]

    sel = jnp.zeros((E, T), F32)
    for hot in hots:
        sel = sel + jnp.where(hot, 1.0, 0.0)
    before = lax.broadcasted_iota(I32, (T, T), 0) < lax.broadcasted_iota(I32, (T, T), 1)
    excl = jnp.dot(sel.astype(BF16), jnp.where(before, 1.0, 0.0).astype(BF16), preferred_element_type=F32)
    base = excl + run_ref[:, 0:1]
    for k in range(TOP_K):
        e_ref[k:k + 1, :] = idxs[k]
        w_ref[k:k + 1, :] = exps[k] / tot
        r_ref[k:k + 1, :] = jnp.sum(jnp.where(hots[k], base, 0.0), axis=0, keepdims=True).astype(I32)
    run_ref[...] = run_ref[...] + jnp.sum(sel, axis=1, keepdims=True)
    cnt_ref[...] = run_ref[...]


def _route(logits, T=512):
    N = logits.shape[0]
    return pl.pallas_call(
        _route_body,
        grid=(N // T,),
        in_specs=[pl.BlockSpec((T, LANES), lambda i: (i, 0))],
        out_specs=[pl.BlockSpec((TOP_K, T), lambda i: (0, i)),
                   pl.BlockSpec((TOP_K, T), lambda i: (0, i)),
                   pl.BlockSpec((TOP_K, T), lambda i: (0, i)),
                   pl.BlockSpec((N_EXPERTS, LANES), lambda i: (0, 0))],
        out_shape=[jax.ShapeDtypeStruct((TOP_K, N), I32),
                   jax.ShapeDtypeStruct((TOP_K, N), F32),
                   jax.ShapeDtypeStruct((TOP_K, N), I32),
                   jax.ShapeDtypeStruct((N_EXPERTS, LANES), F32)],
        scratch_shapes=[pltpu.VMEM((N_EXPERTS, LANES), F32)],
        compiler_params=_cparams("arbitrary"),
        name="route",
    )(logits)


def _dest_body(e_ref, r_ref, ps_ref, d_ref):
    T = e_ref.shape[1]
    eidx = lax.broadcasted_iota(I32, (N_EXPERTS, T), 0)
    starts = ps_ref[:, 0:1]
    for k in range(TOP_K):
        base = jnp.sum(jnp.where(eidx == e_ref[k:k + 1, :], starts, 0.0), axis=0, keepdims=True)
        d_ref[k:k + 1, :] = r_ref[k:k + 1, :] + base.astype(I32)


def _dest_rows(top_e, rank, pad_starts_f, T=2048):
    N = top_e.shape[1]
    return pl.pallas_call(
        _dest_body,
        grid=(N // T,),
        in_specs=[pl.BlockSpec((TOP_K, T), lambda i: (0, i)),
                  pl.BlockSpec((TOP_K, T), lambda i: (0, i)),
                  pl.BlockSpec((N_EXPERTS, LANES), lambda i: (0, 0))],
        out_specs=pl.BlockSpec((TOP_K, T), lambda i: (0, i)),
        out_shape=jax.ShapeDtypeStruct((TOP_K, N), I32),
        compiler_params=_cparams("arbitrary"),
        name="dest_rows",
    )(top_e, rank, pad_starts_f)


def _dispatch_body(zf_ref, dest_ref, h_ref, xs_hbm, zeros_ref, sem, zsem):
    tc = h_ref.shape[0]
    tm = zeros_ref.shape[0]

    @pl.when(pl.program_id(0) == 0)
    def _():
        zeros_ref[...] = jnp.zeros(zeros_ref.shape, zeros_ref.dtype)

        def tile_copy(i):
            return pltpu.make_async_copy(zeros_ref, xs_hbm.at[pl.ds(pl.multiple_of(i * tm, tm), tm), :], zsem)

        def zero_issue(i, carry):
            @pl.when(zf_ref[i] != 0)
            def _():
                tile_copy(i).start()
            return carry

        def zero_drain(i, carry):
            @pl.when(zf_ref[i] != 0)
            def _():
                tile_copy(i).wait()
            return carry

        lax.fori_loop(0, zf_ref.shape[0], zero_issue, 0)
        lax.fori_loop(0, zf_ref.shape[0], zero_drain, 0)

    def row_copy(t, k):
        return pltpu.make_async_copy(h_ref.at[pl.ds(t, 1), :], xs_hbm.at[pl.ds(dest_ref[k, t], 1), :], sem)

    def issue(t, carry):
        for k in range(TOP_K):
            row_copy(t, k).start()
        return carry

    def drain(t, carry):
        for k in range(TOP_K):
            row_copy(t, k).wait()
        return carry

    lax.fori_loop(0, tc, issue, 0)
    lax.fori_loop(0, tc, drain, 0)


def _dispatch(zero_tiles, dest, h2, n_rows, tm, tc=256):
    N, D = h2.shape
    grid_spec = pltpu.PrefetchScalarGridSpec(
        num_scalar_prefetch=1,
        grid=(N // tc,),
        in_specs=[pl.BlockSpec((TOP_K, tc), lambda i, zf: (0, i), memory_space=pltpu.SMEM),
                  pl.BlockSpec((tc, D), lambda i, zf: (i, 0))],
        out_specs=pl.BlockSpec(memory_space=pl.ANY),
        scratch_shapes=[pltpu.VMEM((tm, D), h2.dtype),
                        pltpu.SemaphoreType.DMA(()),
                        pltpu.SemaphoreType.DMA(())],
    )
    return pl.pallas_call(
        _dispatch_body,
        grid_spec=grid_spec,
        out_shape=jax.ShapeDtypeStruct((n_rows, D), h2.dtype),
        compiler_params=_cparams("arbitrary"),
        name="dispatch",
    )(zero_tiles, dest, h2)


PAIR_CHUNK = 2 * LANES
UP_CHUNK = 1024


def _new_weights(te_ref, i):
    return (i == 0) | (te_ref[i] != te_ref[jnp.maximum(i - 1, 0)])


def _ffn_up_body(te_ref, nv_ref, x_ref, w_ref, b_ref, sel_ref, o_ref, wb_ref):
    i = pl.program_id(1)
    valid = i < nv_ref[0]
    fresh = _new_weights(te_ref, i)

    def compute(fresh_weights):
        xb = x_ref[...].astype(BF16)
        for c in range(wb_ref.shape[1] // UP_CHUNK):
            cols = slice(c * UP_CHUNK, (c + 1) * UP_CHUNK)
            if fresh_weights:
                wc = w_ref[0, :, cols].astype(BF16)
                wb_ref[:, cols] = wc
            else:
                wc = wb_ref[:, cols]
            gu = jnp.dot(xb, wc, preferred_element_type=F32) + b_ref[0, :, cols]
            glu = jnp.minimum(gu, SWIGLU_LIMIT)
            half = 0.5 * glu
            gate = half + half * jnp.tanh(SWIGLU_ALPHA * half)
            lin = jnp.clip(gu, -SWIGLU_LIMIT, SWIGLU_LIMIT) + 1.0
            for p in range(UP_CHUNK // PAIR_CHUNK):
                parts = []
                for v in range(PAIR_CHUNK // LANES):
                    lo = p * PAIR_CHUNK + v * LANES
                    parts.append(gate[:, lo:lo + LANES] * pltpu.roll(lin[:, lo:lo + LANES], LANES - 1, 1))
                prod = jnp.concatenate(parts, axis=1).astype(BF16)
                out_lo = (c * UP_CHUNK + p * PAIR_CHUNK) // 2
                o_ref[:, out_lo:out_lo + LANES] = jnp.dot(prod, sel_ref[...], preferred_element_type=F32).astype(BF16)

    @pl.when(valid & fresh)
    def _():
        compute(True)

    @pl.when(valid & jnp.logical_not(fresh))
    def _():
        compute(False)

    @pl.when(jnp.logical_not(valid))
    def _():
        o_ref[...] = jnp.zeros(o_ref.shape, o_ref.dtype)


def _ffn_up(tile_e, n_valid, xs, w_gate_up, b_gate_up, tm, tf=1024):
    P, D = xs.shape
    E, _, F2 = w_gate_up.shape
    F = F2 // 2
    sel = (jnp.arange(PAIR_CHUNK)[:, None] == 2 * jnp.arange(LANES)[None, :]).astype(BF16)

    def row(i, nv):
        return jnp.minimum(i, nv[0] - 1)

    grid_spec = pltpu.PrefetchScalarGridSpec(
        num_scalar_prefetch=2,
        grid=(F // tf, P // tm),
        in_specs=[pl.BlockSpec((tm, D), lambda j, i, te, nv: (row(i, nv), 0)),
                  pl.BlockSpec((1, D, 2 * tf), lambda j, i, te, nv: (te[row(i, nv)], 0, j)),
                  pl.BlockSpec((1, 1, 2 * tf), lambda j, i, te, nv: (te[row(i, nv)], 0, j)),
                  pl.BlockSpec((PAIR_CHUNK, LANES), lambda j, i, te, nv: (0, 0))],
        out_specs=pl.BlockSpec((tm, tf), lambda j, i, te, nv: (i, j)),
        scratch_shapes=[pltpu.VMEM((D, 2 * tf), BF16)],
    )
    return pl.pallas_call(
        _ffn_up_body,
        grid_spec=grid_spec,
        out_shape=jax.ShapeDtypeStruct((P, F), BF16),
        compiler_params=_cparams("arbitrary", "arbitrary"),
        name="ffn_up",
    )(tile_e, n_valid, xs, w_gate_up, b_gate_up.reshape(E, 1, F2), sel)


def _ffn_down_body(te_ref, nv_ref, a_ref, w_ref, b_ref, o_ref, wb_ref):
    i = pl.program_id(1)
    valid = i < nv_ref[0]
    fresh = _new_weights(te_ref, i)

    @pl.when(valid & fresh)
    def _():
        wc = w_ref[0].astype(BF16)
        wb_ref[...] = wc
        o_ref[...] = jnp.dot(a_ref[...], wc, preferred_element_type=F32) + b_ref[0]

    @pl.when(valid & jnp.logical_not(fresh))
    def _():
        o_ref[...] = jnp.dot(a_ref[...], wb_ref[...], preferred_element_type=F32) + b_ref[0]

    @pl.when(jnp.logical_not(valid))
    def _():
        o_ref[...] = jnp.zeros(o_ref.shape, o_ref.dtype)


def _ffn_down(tile_e, n_valid, act, w_down, b_down, tm, tn=2048):
    P, F = act.shape
    E, _, D = w_down.shape

    def row(i, nv):
        return jnp.minimum(i, nv[0] - 1)

    grid_spec = pltpu.PrefetchScalarGridSpec(
        num_scalar_prefetch=2,
        grid=(D // tn, P // tm),
        in_specs=[pl.BlockSpec((tm, F), lambda j, i, te, nv: (row(i, nv), 0)),
                  pl.BlockSpec((1, F, tn), lambda j, i, te, nv: (te[row(i, nv)], 0, j)),
                  pl.BlockSpec((1, 1, tn), lambda j, i, te, nv: (te[row(i, nv)], 0, j))],
        out_specs=pl.BlockSpec((tm, tn), lambda j, i, te, nv: (i, j)),
        scratch_shapes=[pltpu.VMEM((F, tn), BF16)],
    )
    return pl.pallas_call(
        _ffn_down_body,
        grid_spec=grid_spec,
        out_shape=jax.ShapeDtypeStruct((P, D), F32),
        compiler_params=_cparams("arbitrary", "arbitrary"),
        name="ffn_down",
    )(tile_e, n_valid, act, w_down, b_down.reshape(E, 1, D))


def _combine_body(dcur_ref, dnxt_ref, w_ref, x1_ref, mod_ref, g_ref, y_hbm, o_ref, buf_ref, sem_ref):
    i = pl.program_id(0)
    n = pl.num_programs(0)
    tt = x1_ref.shape[0]
    slot = i % 2

    def row_copy(d_ref, s, t, k):
        return pltpu.make_async_copy(y_hbm.at[pl.ds(d_ref[k, t], 1), :],
                                     buf_ref.at[s, k, pl.ds(t, 1), :], sem_ref.at[s])

    def issue(d_ref, s):
        def body(t, carry):
            for k in range(TOP_K):
                row_copy(d_ref, s, t, k).start()
            return carry
        lax.fori_loop(0, tt, body, 0)

    @pl.when(i == 0)
    def _():
        issue(dcur_ref, 0)

    @pl.when(i + 1 < n)
    def _():
        issue(dnxt_ref, 1 - slot)

    def drain(t, carry):
        for k in range(TOP_K):
            row_copy(dcur_ref, slot, t, k).wait()
        return carry
    lax.fori_loop(0, tt, drain, 0)

    acc = jnp.zeros(x1_ref.shape, F32)
    for k in range(TOP_K):
        acc = acc + w_ref[:, k:k + 1] * buf_ref[slot, k]
    o_ref[...] = x1_ref[...] + mod_ref[0, 5:6, :] * _rms(acc, g_ref[...])


def _combine(dest, top_w_t, x1, mod3, g_post, y_rows, rows_per_batch, tt=128):
    N, D = x1.shape
    nt = N // tt
    tpb = rows_per_batch // tt
    return pl.pallas_call(
        _combine_body,
        grid=(nt,),
        in_specs=[pl.BlockSpec((TOP_K, tt), lambda i: (0, i), memory_space=pltpu.SMEM),
                  pl.BlockSpec((TOP_K, tt), lambda i: (0, jnp.minimum(i + 1, nt - 1)), memory_space=pltpu.SMEM),
                  pl.BlockSpec((tt, TOP_K), lambda i: (i, 0)),
                  pl.BlockSpec((tt, D), lambda i: (i, 0)),
                  pl.BlockSpec((1, 6, D), lambda i: (i // tpb, 0, 0)),
                  pl.BlockSpec((1, D), lambda i: (0, 0)),
                  pl.BlockSpec(memory_space=pl.ANY)],
        out_specs=pl.BlockSpec((tt, D), lambda i: (i, 0)),
        out_shape=jax.ShapeDtypeStruct((N, D), F32),
        scratch_shapes=[pltpu.VMEM((2, TOP_K, tt, D), F32),
                        pltpu.SemaphoreType.DMA((2,))],
        compiler_params=_cparams("arbitrary"),
        name="combine",
    )(dest, dest, top_w_t, x1, mod3, g_post, y_rows)


def _block(x, mod3, positions, g_pre_mix, g_post_mix, g_pre_ffn, g_post_ffn, w_in, b_in, conv_qk, sinks, g_mlstm,
           w_attn_proj, w_mlstm_proj, w_out, w_router, b_router, w_gate_up, b_gate_up, w_down, b_down, ffn_tm=512):
    B, S, D = x.shape
    N = B * S
    aq_w = w_attn_proj.shape[0]
    akv_w = (aq_w // ATTN_HEAD_DIM // ATTN_GROUP) * ATTN_HEAD_DIM
    mv_w = w_mlstm_proj.shape[0]
    mqk_w = mv_w // 2
    H = MLSTM_HEADS
    sizes = (aq_w, akv_w, akv_w, mqk_w, mqk_w, mv_w, mv_w, H, H, D, D)
    offs = [0]
    for sz in sizes:
        offs.append(offs[-1] + sz)
    seg_w = lambda k: w_in[:, offs[k]:offs[k + 1]]
    seg_b = lambda k: b_in[offs[k]:offs[k + 1]]
    order = (0, 3, 4, 5, 6, 9, 10)
    w_main = jnp.concatenate([seg_w(k) for k in order], axis=1).astype(BF16)
    b_main = jnp.concatenate([seg_b(k) for k in order]).reshape(1, -1)
    pad = LANES - 2 * H
    w_side = jnp.concatenate([seg_w(1), seg_w(2), seg_w(7), seg_w(8), jnp.zeros((D, pad), F32)], axis=1).astype(BF16)
    b_side = jnp.concatenate([seg_b(1), seg_b(2), seg_b(7), seg_b(8), jnp.zeros((pad,), F32)]).reshape(1, -1)

    x2 = x.reshape(N, D)
    proj, side = _in_proj(x2, mod3, g_pre_mix.reshape(1, D), w_main, b_main, w_side, b_side, S)

    c_mq = aq_w
    c_mv = aq_w + 2 * mqk_w
    c_mo = c_mv + mv_w
    c_ga = c_mo + mv_w
    c_gm = c_ga + D
    side_k, side_v, side_gates = 0, akv_w // LANES, 2 * akv_w // LANES

    half = ROPE_DIM // 2
    inv_freq = ROPE_THETA ** (-jnp.arange(half, dtype=F32) / half)
    lane_d = jnp.arange(LANES) % ATTN_HEAD_DIM
    inv_lane = jnp.where(lane_d < ROPE_DIM, inv_freq[lane_d % half], 0.0).reshape(1, LANES).astype(F32)
    pos_f = positions.astype(F32).reshape(N, 1)
    attn_o, mls_o = _mixers(proj, side, pos_f, inv_lane, sinks, conv_qk, g_mlstm.reshape(1, mv_w), B, S,
                            aq_w, side_k, side_v, c_mq // (2 * mqk_w), c_mv // mv_w, c_mo // mv_w, side_gates,
                            2 * mqk_w, mv_w)

    E = N_EXPERTS
    wr = jnp.concatenate([w_router, jnp.zeros((D, LANES - E), F32)], axis=1)
    br = jnp.concatenate([b_router, jnp.full((LANES - E,), -1e30, F32)]).reshape(1, LANES)
    merged = _merge(attn_o, mls_o, proj, w_attn_proj.astype(BF16), w_mlstm_proj.astype(BF16), c_ga // D, c_gm // D)
    x1, h2, logits = _out_proj(merged, x2, mod3, g_post_mix.reshape(1, D), g_pre_ffn.reshape(1, D),
                               w_out.astype(BF16), wr, br, S)

    top_e, top_w, rank, counts = _route(logits)
    cnt = counts[:, 0].astype(I32)
    padded = (cnt + ffn_tm - 1) // ffn_tm * ffn_tm
    pad_ends = jnp.cumsum(padded)
    pad_starts = pad_ends - padded
    n_rows = N * TOP_K + E * ffn_tm
    n_tiles = n_rows // ffn_tm
    tile_start = jnp.arange(n_tiles, dtype=I32) * ffn_tm
    tile_e = jnp.minimum(jnp.sum((pad_ends[None, :] <= tile_start[:, None]).astype(I32), axis=1), E - 1)
    n_valid = (pad_ends[E - 1:E] // ffn_tm).astype(I32)
    dest = _dest_rows(top_e, rank, jnp.broadcast_to(pad_starts.astype(F32)[:, None], (E, LANES)))

    tile_end = tile_start + ffn_tm
    zero_tiles = (jnp.any(pad_ends[None, :] == tile_end[:, None], axis=1)
                  | (tile_start >= pad_ends[E - 1])).astype(I32)
    xs = _dispatch(zero_tiles, dest, h2, n_rows, ffn_tm)
    act = _ffn_up(tile_e, n_valid, xs, w_gate_up, b_gate_up, ffn_tm)
    y_rows = _ffn_down(tile_e, n_valid, act, w_down, b_down, ffn_tm)
    out = _combine(dest, top_w.T, x1, mod3, g_post_ffn.reshape(1, D), y_rows, S)
    return out.reshape(B, S, D)


def kernel(x, c, positions, w_ada, b_ada, g_pre_mix, g_post_mix, g_pre_ffn, g_post_ffn, w_in, b_in, conv_qk, sinks,
           g_mlstm, w_attn_proj, w_mlstm_proj, w_out, w_router, b_router, w_gate_up, b_gate_up, w_down, b_down):
    B, S, D = x.shape
    depth = w_ada.shape[0]
    for l in range(depth):
        mod3 = _ada_ln(c, w_ada[l], b_ada[l]).reshape(B, 6, D)
        x = _block(x, mod3, positions, g_pre_mix[l], g_post_mix[l], g_pre_ffn[l], g_post_ffn[l], w_in[l], b_in[l],
                   conv_qk[l], sinks[l], g_mlstm[l], w_attn_proj[l], w_mlstm_proj[l], w_out[l], w_router[l],
                   b_router[l], w_gate_up[l], b_gate_up[l], w_down[l], b_down[l])
    return x
```

```python
import jax
import jax.numpy as jnp
from jax import lax
from jax.experimental import pallas as pl
from jax.experimental.pallas import tpu as pltpu

F32 = jnp.float32
BF16 = jnp.bfloat16
I32 = jnp.int32
HIGHEST = lax.Precision.HIGHEST

ATTN_HEAD_DIM = 64
ATTN_GROUP = 8
WINDOW = 128
ATTN_BLOCK = 128
ROPE_THETA = 500000.0
ROPE_DIM = ATTN_HEAD_DIM // 4
MLSTM_HEADS = 4
MLSTM_CHUNK = 128
CONV_WIDTH = 4
GATE_SOFTCAP = 15.0
N_EXPERTS = 32
TOP_K = 4
SWIGLU_ALPHA = 1.702
SWIGLU_LIMIT = 7.0
NORM_EPS = 1e-6

LANES = 128
SUBLANES = 8
VMEM_LIMIT = 60 * 1024 * 1024

NEG_INF = float("-inf")


def _cparams(*sem):
    return pltpu.CompilerParams(dimension_semantics=sem, vmem_limit_bytes=VMEM_LIMIT)


def _rms(x, g):
    ms = jnp.mean(x * x, axis=-1, keepdims=True)
    return x * lax.rsqrt(ms + NORM_EPS) * g


def _sigmoid(x):
    return 1.0 / (1.0 + jnp.exp(-x))


def _ada_body(c_ref, w_ref, b_ref, o_ref):
    c = c_ref[...]
    cond = c * _sigmoid(c)
    o_ref[...] = jnp.dot(cond, w_ref[...], preferred_element_type=F32, precision=HIGHEST) + b_ref[...]


def _ada_ln(c, w_ada, b_ada):
    B, D = c.shape
    W = w_ada.shape[1]
    tn = 1024
    cp = jnp.zeros((SUBLANES, D), F32).at[:B].set(c)
    out = pl.pallas_call(
        _ada_body,
        grid=(W // tn,),
        in_specs=[pl.BlockSpec((SUBLANES, D), lambda j: (0, 0)),
                  pl.BlockSpec((D, tn), lambda j: (0, j)),
                  pl.BlockSpec((1, tn), lambda j: (0, j))],
        out_specs=pl.BlockSpec((SUBLANES, tn), lambda j: (0, j)),
        out_shape=jax.ShapeDtypeStruct((SUBLANES, W), F32),
        compiler_params=_cparams("arbitrary"),
        name="ada_ln",
    )(cp, w_ada, b_ada.reshape(1, W))
    return out[:B]


def _inproj_body(x_ref, mod_ref, g_ref, w_ref, b_ref, wg_ref, bg_ref, o_ref, og_ref, h_ref):
    @pl.when(pl.program_id(1) == 0)
    def _():
        h = _rms(x_ref[...], g_ref[...]) * (1.0 + mod_ref[0, 1:2, :]) + mod_ref[0, 0:1, :]
        hb = h.astype(BF16)
        h_ref[...] = hb
        og_ref[...] = jnp.dot(hb, wg_ref[...], preferred_element_type=F32) + bg_ref[...]

    acc = jnp.dot(h_ref[...], w_ref[...], preferred_element_type=F32) + b_ref[...]
    o_ref[...] = acc.astype(BF16)


def _in_proj(x2, mod3, g, w_main, b_main, w_side, b_side, rows_per_batch, tm=1024, tn=1024):
    N, D = x2.shape
    W = w_main.shape[1]
    WS = w_side.shape[1]
    tpb = rows_per_batch // tm
    return pl.pallas_call(
        _inproj_body,
        grid=(N // tm, W // tn),
        in_specs=[pl.BlockSpec((tm, D), lambda i, j: (i, 0)),
                  pl.BlockSpec((1, 6, D), lambda i, j: (i // tpb, 0, 0)),
                  pl.BlockSpec((1, D), lambda i, j: (0, 0)),
                  pl.BlockSpec((D, tn), lambda i, j: (0, j)),
                  pl.BlockSpec((1, tn), lambda i, j: (0, j)),
                  pl.BlockSpec((D, WS), lambda i, j: (0, 0)),
                  pl.BlockSpec((1, WS), lambda i, j: (0, 0))],
        out_specs=[pl.BlockSpec((tm, tn), lambda i, j: (i, j)),
                   pl.BlockSpec((tm, WS), lambda i, j: (i, 0))],
        out_shape=[jax.ShapeDtypeStruct((N, W), BF16),
                   jax.ShapeDtypeStruct((N, WS), F32)],
        scratch_shapes=[pltpu.VMEM((tm, D), BF16)],
        compiler_params=_cparams("arbitrary", "arbitrary"),
        name="in_proj",
    )(x2, mod3, g, w_main, b_main, w_side, b_side)


def _attn_compute(n, sinks_ref, q_ref, kc_ref, kp_ref, vc_ref, vp_ref, cos_c, sin_c, cos_p, sin_p, o_ref):
    L = ATTN_BLOCK
    lane = lax.broadcasted_iota(I32, (1, LANES), 1)
    first_half = (lane % ATTN_HEAD_DIM) < (ROPE_DIM // 2)
    low_lanes = lane < ATTN_HEAD_DIM

    def rope(x, cs, sn):
        swapped = jnp.where(first_half, -pltpu.roll(x, LANES - ROPE_DIM // 2, 1), pltpu.roll(x, ROPE_DIM // 2, 1))
        return x * cs + swapped * sn

    k2 = jnp.concatenate([rope(kp_ref[...].astype(F32), cos_p, sin_p),
                          rope(kc_ref[...].astype(F32), cos_c, sin_c)], axis=0)
    v2 = jnp.concatenate([vp_ref[...], vc_ref[...]], axis=0).astype(F32)
    k2s = pltpu.roll(k2, ATTN_HEAD_DIM, 1)
    v2s = pltpu.roll(v2, ATTN_HEAD_DIM, 1)
    zero = jnp.zeros_like(k2)
    k_lo = [jnp.where(low_lanes, k2, zero).astype(BF16), jnp.where(low_lanes, k2s, zero).astype(BF16)]
    k_hi = [jnp.where(low_lanes, zero, k2s).astype(BF16), jnp.where(low_lanes, zero, k2).astype(BF16)]
    v_lo = [jnp.where(low_lanes, v2, zero).astype(BF16), jnp.where(low_lanes, v2s, zero).astype(BF16)]
    v_hi = [jnp.where(low_lanes, zero, v2s).astype(BF16), jnp.where(low_lanes, zero, v2).astype(BF16)]

    qi = lax.broadcasted_iota(I32, (L, 2 * L), 0)
    kj = lax.broadcasted_iota(I32, (L, 2 * L), 1)
    diff = qi + L - kj
    valid = (diff >= 0) & (diff < WINDOW) & ((kj >= L) | (n > 0))

    scale = ATTN_HEAD_DIM ** -0.5
    n_pairs = q_ref.shape[1] // LANES
    for j in range(n_pairs):
        hk = (2 * j) // ATTN_GROUP
        qp = (rope(q_ref[:, j * LANES:(j + 1) * LANES].astype(F32), cos_c, sin_c) * scale).astype(BF16)
        acc = jnp.zeros((L, LANES), F32)
        for half, (kx, vx) in enumerate(((k_lo[hk], v_lo[hk]), (k_hi[hk], v_hi[hk]))):
            s = lax.dot_general(qp, kx, (((1,), (1,)), ((), ())), preferred_element_type=F32)
            s = jnp.where(valid, s, NEG_INF)
            sink = sinks_ref[2 * j + half]
            m = jnp.maximum(jnp.max(s, axis=-1, keepdims=True), sink)
            p = jnp.exp(s - m)
            denom = jnp.sum(p, axis=-1, keepdims=True) + jnp.exp(sink - m)
            acc = acc + jnp.dot(p.astype(BF16), vx, preferred_element_type=F32) / denom
        o_ref[:, j * LANES:(j + 1) * LANES] = acc.astype(BF16)


def _log_sigmoid(z):
    return jnp.minimum(z, 0.0) - jnp.log1p(jnp.exp(-jnp.abs(z)))


def _mlstm_compute(qk_ref, v_ref, og_ref, gates_ref, cw_ref, gn_ref, out_ref, hist_ref, c_ref, m_ref):
    L = MLSTM_CHUNK
    H = MLSTM_HEADS
    dqk = qk_ref.shape[1] // (2 * H)
    dv = v_ref.shape[1] // H
    tail = SUBLANES

    hist_ref[tail:tail + L, :] = qk_ref[...].astype(F32)
    conv = hist_ref[tail:tail + L, :] * cw_ref[CONV_WIDTH - 1:CONV_WIDTH, :]
    for t in range(CONV_WIDTH - 1):
        off = tail - (CONV_WIDTH - 1) + t
        conv = conv + hist_ref[off:off + L, :] * cw_ref[t:t + 1, :]
    hist_ref[0:tail, :] = hist_ref[L:L + tail, :]
    qk = conv * _sigmoid(conv)

    gates = gates_ref[...]
    capped = GATE_SOFTCAP * jnp.tanh(gates / GATE_SOFTCAP)
    log_f = _log_sigmoid(capped)
    row = lax.broadcasted_iota(I32, (L, L), 0)
    col = lax.broadcasted_iota(I32, (L, L), 1)
    causal = col <= row
    b_all = jnp.dot(causal.astype(F32), log_f, preferred_element_type=F32, precision=HIGHEST)
    b_all_t = b_all.T
    capped_t = capped.T
    ones_col = jnp.where(lax.broadcasted_iota(I32, (L, LANES), 1) == 0, 1.0, 0.0).astype(BF16)

    for h in range(H):
        b_col = b_all[:, H + h:H + h + 1]
        li_col = capped[:, h:h + 1]
        b_row = b_all_t[H + h:H + h + 1, :]
        li_row = capped_t[h:h + 1, :]
        g_last = b_all[L - 1:L, H + h:H + h + 1]
        m_prev = m_ref[h:h + 1, 0:1]

        d_log = jnp.where(causal, b_col - b_row + li_row, NEG_INF)
        inter = b_col + m_prev
        m = jnp.maximum(inter, jnp.max(d_log, axis=-1, keepdims=True))
        q_h = qk[:, h * dqk:(h + 1) * dqk].astype(BF16)
        k_f = qk[:, (H + h) * dqk:(H + h + 1) * dqk] * (dqk ** -0.5)
        s_qk = lax.dot_general(q_h, k_f.astype(BF16), (((1,), (1,)), ((), ())),
                               preferred_element_type=F32) * jnp.exp(d_log - m)
        v_ext = jnp.concatenate([v_ref[:, h * dv:(h + 1) * dv], ones_col], axis=1)
        c_prev = c_ref[h]
        tot = (jnp.dot(s_qk.astype(BF16), v_ext, preferred_element_type=F32)
               + jnp.exp(inter - m) * jnp.dot(q_h, c_prev.astype(BF16), preferred_element_type=F32))
        h_t = tot[:, :dv] / jnp.maximum(jnp.abs(tot[:, dv:dv + 1]), jnp.exp(-m))

        a_col = g_last - b_col + li_col
        m_new = jnp.maximum(g_last + m_prev, jnp.max(a_col, axis=0, keepdims=True))
        wk = jnp.exp(a_col - m_new) * k_f
        kv = jnp.dot(wk.T.astype(BF16), v_ext, preferred_element_type=F32)
        c_ref[h] = jnp.exp(g_last + m_prev - m_new) * c_prev + kv
        m_ref[h:h + 1, :] = jnp.broadcast_to(m_new, (1, LANES))

        hn = _rms(h_t, gn_ref[:, h * dv:(h + 1) * dv])
        out_ref[:, h * dv:(h + 1) * dv] = (_sigmoid(og_ref[:, h * dv:(h + 1) * dv].astype(F32)) * hn).astype(BF16)


def _mixers_body(sinks_ref, q_ref, kc_ref, kp_ref, vc_ref, vp_ref, pos_ref, inv_ref,
                 qk_ref, v_ref, og_ref, gates_ref, cw_ref, gn_ref,
                 attn_ref, mls_ref, trig_ref, hist_ref, c_ref, m_ref):
    n = pl.program_id(1)
    ang = pos_ref[...] * inv_ref[...]
    cos_c, sin_c = jnp.cos(ang), jnp.sin(ang)

    @pl.when(n == 0)
    def _():
        trig_ref[0] = cos_c
        trig_ref[1] = sin_c
        hist_ref[0:SUBLANES, :] = jnp.zeros((SUBLANES, hist_ref.shape[1]), F32)
        c_ref[...] = jnp.zeros(c_ref.shape, F32)
        m_ref[...] = jnp.zeros(m_ref.shape, F32)

    cos_p, sin_p = trig_ref[0], trig_ref[1]
    _attn_compute(n, sinks_ref, q_ref, kc_ref, kp_ref, vc_ref, vp_ref, cos_c, sin_c, cos_p, sin_p, attn_ref)
    trig_ref[0] = cos_c
    trig_ref[1] = sin_c
    _mlstm_compute(qk_ref, v_ref, og_ref, gates_ref, cw_ref, gn_ref, mls_ref, hist_ref, c_ref, m_ref)


def _mixers(proj, side, pos_f, inv_lane, sinks, conv_w, g_norm, B, S, q_width, k_blk, v_blk,
            qk_blk, mv_blk, og_blk, gates_blk, qk_width, v_width):
    L = ATTN_BLOCK
    assert MLSTM_CHUNK == L
    nb = S // L
    dv = v_width // MLSTM_HEADS
    dqk = qk_width // (2 * MLSTM_HEADS)

    def cur(b, n, s):
        return b * nb + n

    def prev(b, n, s):
        return b * nb + jnp.maximum(n - 1, 0)

    grid_spec = pltpu.PrefetchScalarGridSpec(
        num_scalar_prefetch=1,
        grid=(B, nb),
        in_specs=[pl.BlockSpec((L, q_width), lambda b, n, s: (cur(b, n, s), 0)),
                  pl.BlockSpec((L, LANES), lambda b, n, s: (cur(b, n, s), k_blk)),
                  pl.BlockSpec((L, LANES), lambda b, n, s: (prev(b, n, s), k_blk)),
                  pl.BlockSpec((L, LANES), lambda b, n, s: (cur(b, n, s), v_blk)),
                  pl.BlockSpec((L, LANES), lambda b, n, s: (prev(b, n, s), v_blk)),
                  pl.BlockSpec((L, 1), lambda b, n, s: (cur(b, n, s), 0)),
                  pl.BlockSpec((1, LANES), lambda b, n, s: (0, 0)),
                  pl.BlockSpec((L, qk_width), lambda b, n, s: (cur(b, n, s), qk_blk)),
                  pl.BlockSpec((L, v_width), lambda b, n, s: (cur(b, n, s), mv_blk)),
                  pl.BlockSpec((L, v_width), lambda b, n, s: (cur(b, n, s), og_blk)),
                  pl.BlockSpec((L, LANES), lambda b, n, s: (cur(b, n, s), gates_blk)),
                  pl.BlockSpec((CONV_WIDTH, qk_width), lambda b, n, s: (0, 0)),
                  pl.BlockSpec((1, v_width), lambda b, n, s: (0, 0))],
        out_specs=[pl.BlockSpec((L, q_width), lambda b, n, s: (cur(b, n, s), 0)),
                   pl.BlockSpec((L, v_width), lambda b, n, s: (cur(b, n, s), 0))],
        scratch_shapes=[pltpu.VMEM((2, L, LANES), F32),
                        pltpu.VMEM((L + 2 * SUBLANES, qk_width), F32),
                        pltpu.VMEM((MLSTM_HEADS, dqk, dv + LANES), F32),
                        pltpu.VMEM((SUBLANES, LANES), F32)],
    )
    return pl.pallas_call(
        _mixers_body,
        grid_spec=grid_spec,
        out_shape=[jax.ShapeDtypeStruct((B * S, q_width), BF16),
                   jax.ShapeDtypeStruct((B * S, v_width), BF16)],
        compiler_params=_cparams("arbitrary", "arbitrary"),
        name="mixers",
    )(sinks, proj, side, side, side, side, pos_f, inv_lane, proj, proj, proj, side, conv_w, g_norm)


def _merge_body(a_ref, m_ref, ga_ref, gm_ref, wa_ref, wm_ref, o_ref):
    ya = jnp.dot(a_ref[...], wa_ref[...], preferred_element_type=F32)
    ym = jnp.dot(m_ref[...], wm_ref[...], preferred_element_type=F32)
    merged = _sigmoid(ga_ref[...].astype(F32)) * ya + _sigmoid(gm_ref[...].astype(F32)) * ym
    o_ref[...] = merged.astype(BF16)


def _merge(attn_o, mls_o, proj, wa, wm, ga_blk, gm_blk, tm=1024):
    N = attn_o.shape[0]
    D = wa.shape[1]
    wa_w, wm_w = attn_o.shape[1], mls_o.shape[1]
    const = lambda i: (0, 0)
    return pl.pallas_call(
        _merge_body,
        grid=(N // tm,),
        in_specs=[pl.BlockSpec((tm, wa_w), lambda i: (i, 0)),
                  pl.BlockSpec((tm, wm_w), lambda i: (i, 0)),
                  pl.BlockSpec((tm, D), lambda i: (i, ga_blk)),
                  pl.BlockSpec((tm, D), lambda i: (i, gm_blk)),
                  pl.BlockSpec((wa_w, D), const),
                  pl.BlockSpec((wm_w, D), const)],
        out_specs=pl.BlockSpec((tm, D), lambda i: (i, 0)),
        out_shape=jax.ShapeDtypeStruct((N, D), BF16),
        compiler_params=_cparams("arbitrary"),
        name="merge",
    )(attn_o, mls_o, proj, proj, wa, wm)


def _split_bf16(v):
    hi = v.astype(BF16)
    return hi, (v - hi.astype(F32)).astype(BF16)


def _outproj_body(mg_ref, x_ref, mod_ref, gpost_ref, gpre_ref, wo_ref, wr_ref, br_ref, x1_ref, h2_ref, lg_ref):
    y = jnp.dot(mg_ref[...], wo_ref[...], preferred_element_type=F32)
    x1 = x_ref[...] + mod_ref[0, 2:3, :] * _rms(y, gpost_ref[...])
    x1_ref[...] = x1
    h2 = _rms(x1, gpre_ref[...]) * (1.0 + mod_ref[0, 4:5, :]) + mod_ref[0, 3:4, :]
    h2_ref[...] = h2
    h_hi, h_lo = _split_bf16(h2)
    w_hi, w_lo = _split_bf16(wr_ref[...])
    lg = (jnp.dot(h_hi, w_hi, preferred_element_type=F32) + jnp.dot(h_hi, w_lo, preferred_element_type=F32)
          + jnp.dot(h_lo, w_hi, preferred_element_type=F32))
    lg_ref[...] = lg + br_ref[...]


def _out_proj(merged, x2, mod3, g_post, g_pre, wo, wr, br, rows_per_batch, tm=512):
    N, D = x2.shape
    tpb = rows_per_batch // tm
    const = lambda i: (0, 0)
    return pl.pallas_call(
        _outproj_body,
        grid=(N // tm,),
        in_specs=[pl.BlockSpec((tm, D), lambda i: (i, 0)),
                  pl.BlockSpec((tm, D), lambda i: (i, 0)),
                  pl.BlockSpec((1, 6, D), lambda i: (i // tpb, 0, 0)),
                  pl.BlockSpec((1, D), const),
                  pl.BlockSpec((1, D), const),
                  pl.BlockSpec((D, D), const),
                  pl.BlockSpec((D, LANES), const),
                  pl.BlockSpec((1, LANES), const)],
        out_specs=[pl.BlockSpec((tm, D), lambda i: (i, 0)),
                   pl.BlockSpec((tm, D), lambda i: (i, 0)),
                   pl.BlockSpec((tm, LANES), lambda i: (i, 0))],
        out_shape=[jax.ShapeDtypeStruct((N, D), F32),
                   jax.ShapeDtypeStruct((N, D), F32),
                   jax.ShapeDtypeStruct((N, LANES), F32)],
        compiler_params=_cparams("arbitrary"),
        name="out_proj",
    )(merged, x2, mod3, g_post, g_pre, wo, wr, br)


def _route_body(lg_ref, e_ref, w_ref, r_ref, cnt_ref, run_ref):
    T = lg_ref.shape[0]
    E = N_EXPERTS

    @pl.when(pl.program_id(0) == 0)
    def _():
        run_ref[...] = jnp.zeros(run_ref.shape, F32)

    lg = lg_ref[...].T[0:E, :]
    eidx = lax.broadcasted_iota(I32, (E, T), 0)
    tops, idxs, hots = [], [], []
    for _ in range(TOP_K):
        mx = jnp.max(lg, axis=0, keepdims=True)
        ik = jnp.min(jnp.where(lg == mx, eidx, E), axis=0, keepdims=True)
        hot = eidx == ik
        lg = jnp.where(hot, NEG_INF, lg)
        tops.append(mx)
        idxs.append(ik)
        hots.append(hot)
    exps = [jnp.exp(t - tops[0]) for t in tops]
    tot = exps[0] + exps[1] + exps[2] + exps[3]

    sel = jnp.zeros((E, T), F32)
    for hot in hots:
        sel = sel + jnp.where(hot, 1.0, 0.0)
    before = lax.broadcasted_iota(I32, (T, T), 0) < lax.broadcasted_iota(I32, (T, T), 1)
    excl = jnp.dot(sel.astype(BF16), jnp.where(before, 1.0, 0.0).astype(BF16), preferred_element_type=F32)
    base = excl + run_ref[:, 0:1]
    for k in range(TOP_K):
        e_ref[k:k + 1, :] = idxs[k]
        w_ref[k:k + 1, :] = exps[k] / tot
        r_ref[k:k + 1, :] = jnp.sum(jnp.where(hots[k], base, 0.0), axis=0, keepdims=True).astype(I32)
    run_ref[...] = run_ref[...] + jnp.sum(sel, axis=1, keepdims=True)
    cnt_ref[...] = run_ref[...]


def _route(logits, T=512):
    N = logits.shape[0]
    return pl.pallas_call(
        _route_body,
        grid=(N // T,),
        in_specs=[pl.BlockSpec((T, LANES), lambda i: (i, 0))],
        out_specs=[pl.BlockSpec((TOP_K, T), lambda i: (0, i)),
                   pl.BlockSpec((TOP_K, T), lambda i: (0, i)),
                   pl.BlockSpec((TOP_K, T), lambda i: (0, i)),
                   pl.BlockSpec((N_EXPERTS, LANES), lambda i: (0, 0))],
        out_shape=[jax.ShapeDtypeStruct((TOP_K, N), I32),
                   jax.ShapeDtypeStruct((TOP_K, N), F32),
                   jax.ShapeDtypeStruct((TOP_K, N), I32),
                   jax.ShapeDtypeStruct((N_EXPERTS, LANES), F32)],
        scratch_shapes=[pltpu.VMEM((N_EXPERTS, LANES), F32)],
        compiler_params=_cparams("arbitrary"),
        name="route",
    )(logits)


def _dest_body(e_ref, r_ref, ps_ref, d_ref):
    T = e_ref.shape[1]
    eidx = lax.broadcasted_iota(I32, (N_EXPERTS, T), 0)
    starts = ps_ref[:, 0:1]
    for k in range(TOP_K):
        base = jnp.sum(jnp.where(eidx == e_ref[k:k + 1, :], starts, 0.0), axis=0, keepdims=True)
        d_ref[k:k + 1, :] = r_ref[k:k + 1, :] + base.astype(I32)


def _dest_rows(top_e, rank, pad_starts_f, T=2048):
    N = top_e.shape[1]
    return pl.pallas_call(
        _dest_body,
        grid=(N // T,),
        in_specs=[pl.BlockSpec((TOP_K, T), lambda i: (0, i)),
                  pl.BlockSpec((TOP_K, T), lambda i: (0, i)),
                  pl.BlockSpec((N_EXPERTS, LANES), lambda i: (0, 0))],
        out_specs=pl.BlockSpec((TOP_K, T), lambda i: (0, i)),
        out_shape=jax.ShapeDtypeStruct((TOP_K, N), I32),
        compiler_params=_cparams("arbitrary"),
        name="dest_rows",
    )(top_e, rank, pad_starts_f)


def _dispatch_body(zf_ref, dest_ref, h_ref, xs_hbm, zeros_ref, sem, zsem):
    tc = h_ref.shape[0]
    tm = zeros_ref.shape[0]

    @pl.when(pl.program_id(0) == 0)
    def _():
        zeros_ref[...] = jnp.zeros(zeros_ref.shape, zeros_ref.dtype)

        def tile_copy(i):
            return pltpu.make_async_copy(zeros_ref, xs_hbm.at[pl.ds(pl.multiple_of(i * tm, tm), tm), :], zsem)

        def zero_issue(i, carry):
            @pl.when(zf_ref[i] != 0)
            def _():
                tile_copy(i).start()
            return carry

        def zero_drain(i, carry):
            @pl.when(zf_ref[i] != 0)
            def _():
                tile_copy(i).wait()
            return carry

        lax.fori_loop(0, zf_ref.shape[0], zero_issue, 0)
        lax.fori_loop(0, zf_ref.shape[0], zero_drain, 0)

    def row_copy(t, k):
        return pltpu.make_async_copy(h_ref.at[pl.ds(t, 1), :], xs_hbm.at[pl.ds(dest_ref[k, t], 1), :], sem)

    def issue(t, carry):
        for k in range(TOP_K):
            row_copy(t, k).start(priority=k % 2)
        return carry

    def drain(t, carry):
        for k in range(TOP_K):
            row_copy(t, k).wait()
        return carry

    lax.fori_loop(0, tc, issue, 0)
    lax.fori_loop(0, tc, drain, 0)


def _dispatch(zero_tiles, dest, h2, n_rows, tm, tc=256):
    N, D = h2.shape
    grid_spec = pltpu.PrefetchScalarGridSpec(
        num_scalar_prefetch=1,
        grid=(N // tc,),
        in_specs=[pl.BlockSpec((TOP_K, tc), lambda i, zf: (0, i), memory_space=pltpu.SMEM),
                  pl.BlockSpec((tc, D), lambda i, zf: (i, 0))],
        out_specs=pl.BlockSpec(memory_space=pl.ANY),
        scratch_shapes=[pltpu.VMEM((tm, D), h2.dtype),
                        pltpu.SemaphoreType.DMA(()),
                        pltpu.SemaphoreType.DMA(())],
    )
    return pl.pallas_call(
        _dispatch_body,
        grid_spec=grid_spec,
        out_shape=jax.ShapeDtypeStruct((n_rows, D), h2.dtype),
        compiler_params=_cparams("arbitrary"),
        name="dispatch",
    )(zero_tiles, dest, h2)


PAIR_CHUNK = 2 * LANES
UP_CHUNK = 1024


def _new_weights(te_ref, i):
    return (i == 0) | (te_ref[i] != te_ref[jnp.maximum(i - 1, 0)])


def _ffn_up_body(te_ref, nv_ref, x_ref, w_ref, b_ref, sel_ref, o_ref, wb_ref):
    i = pl.program_id(1)
    valid = i < nv_ref[0]
    fresh = _new_weights(te_ref, i)

    def compute(fresh_weights):
        xb = x_ref[...].astype(BF16)
        for c in range(wb_ref.shape[1] // UP_CHUNK):
            cols = slice(c * UP_CHUNK, (c + 1) * UP_CHUNK)
            if fresh_weights:
                wc = w_ref[0, :, cols].astype(BF16)
                wb_ref[:, cols] = wc
            else:
                wc = wb_ref[:, cols]
            gu = jnp.dot(xb, wc, preferred_element_type=F32) + b_ref[0, :, cols]
            glu = jnp.minimum(gu, SWIGLU_LIMIT)
            half = 0.5 * glu
            gate = half + half * jnp.tanh(SWIGLU_ALPHA * half)
            lin = jnp.clip(gu, -SWIGLU_LIMIT, SWIGLU_LIMIT) + 1.0
            for p in range(UP_CHUNK // PAIR_CHUNK):
                parts = []
                for v in range(PAIR_CHUNK // LANES):
                    lo = p * PAIR_CHUNK + v * LANES
                    parts.append(gate[:, lo:lo + LANES] * pltpu.roll(lin[:, lo:lo + LANES], LANES - 1, 1))
                prod = jnp.concatenate(parts, axis=1).astype(BF16)
                out_lo = (c * UP_CHUNK + p * PAIR_CHUNK) // 2
                o_ref[:, out_lo:out_lo + LANES] = jnp.dot(prod, sel_ref[...], preferred_element_type=F32).astype(BF16)

    @pl.when(valid & fresh)
    def _():
        compute(True)

    @pl.when(valid & jnp.logical_not(fresh))
    def _():
        compute(False)

    @pl.when(jnp.logical_not(valid))
    def _():
        o_ref[...] = jnp.zeros(o_ref.shape, o_ref.dtype)


def _ffn_up(tile_e, n_valid, xs, w_gate_up, b_gate_up, tm, tf=1024):
    P, D = xs.shape
    E, _, F2 = w_gate_up.shape
    F = F2 // 2
    sel = (jnp.arange(PAIR_CHUNK)[:, None] == 2 * jnp.arange(LANES)[None, :]).astype(BF16)

    def row(i, nv):
        return jnp.minimum(i, nv[0] - 1)

    grid_spec = pltpu.PrefetchScalarGridSpec(
        num_scalar_prefetch=2,
        grid=(F // tf, P // tm),
        in_specs=[pl.BlockSpec((tm, D), lambda j, i, te, nv: (row(i, nv), 0)),
                  pl.BlockSpec((1, D, 2 * tf), lambda j, i, te, nv: (te[row(i, nv)], 0, j)),
                  pl.BlockSpec((1, 1, 2 * tf), lambda j, i, te, nv: (te[row(i, nv)], 0, j)),
                  pl.BlockSpec((PAIR_CHUNK, LANES), lambda j, i, te, nv: (0, 0))],
        out_specs=pl.BlockSpec((tm, tf), lambda j, i, te, nv: (i, j)),
        scratch_shapes=[pltpu.VMEM((D, 2 * tf), BF16)],
    )
    return pl.pallas_call(
        _ffn_up_body,
        grid_spec=grid_spec,
        out_shape=jax.ShapeDtypeStruct((P, F), BF16),
        compiler_params=_cparams("arbitrary", "arbitrary"),
        name="ffn_up",
    )(tile_e, n_valid, xs, w_gate_up, b_gate_up.reshape(E, 1, F2), sel)


def _ffn_down_body(te_ref, nv_ref, a_ref, w_ref, b_ref, o_ref, wb_ref):
    i = pl.program_id(1)
    valid = i < nv_ref[0]
    fresh = _new_weights(te_ref, i)

    @pl.when(valid & fresh)
    def _():
        wc = w_ref[0].astype(BF16)
        wb_ref[...] = wc
        o_ref[...] = jnp.dot(a_ref[...], wc, preferred_element_type=F32) + b_ref[0]

    @pl.when(valid & jnp.logical_not(fresh))
    def _():
        o_ref[...] = jnp.dot(a_ref[...], wb_ref[...], preferred_element_type=F32) + b_ref[0]

    @pl.when(jnp.logical_not(valid))
    def _():
        o_ref[...] = jnp.zeros(o_ref.shape, o_ref.dtype)


def _ffn_down(tile_e, n_valid, act, w_down, b_down, tm, tn=2048):
    P, F = act.shape
    E, _, D = w_down.shape

    def row(i, nv):
        return jnp.minimum(i, nv[0] - 1)

    grid_spec = pltpu.PrefetchScalarGridSpec(
        num_scalar_prefetch=2,
        grid=(D // tn, P // tm),
        in_specs=[pl.BlockSpec((tm, F), lambda j, i, te, nv: (row(i, nv), 0)),
                  pl.BlockSpec((1, F, tn), lambda j, i, te, nv: (te[row(i, nv)], 0, j)),
                  pl.BlockSpec((1, 1, tn), lambda j, i, te, nv: (te[row(i, nv)], 0, j))],
        out_specs=pl.BlockSpec((tm, tn), lambda j, i, te, nv: (i, j)),
        scratch_shapes=[pltpu.VMEM((F, tn), BF16)],
    )
    return pl.pallas_call(
        _ffn_down_body,
        grid_spec=grid_spec,
        out_shape=jax.ShapeDtypeStruct((P, D), F32),
        compiler_params=_cparams("arbitrary", "arbitrary"),
        name="ffn_down",
    )(tile_e, n_valid, act, w_down, b_down.reshape(E, 1, D))


def _combine_body(dcur_ref, dnxt_ref, w_ref, x1_ref, mod_ref, g_ref, y_hbm, o_ref, buf_ref, sem_ref):
    i = pl.program_id(0)
    n = pl.num_programs(0)
    tt = x1_ref.shape[0]
    slot = i % 2

    def row_copy(d_ref, s, t, k):
        return pltpu.make_async_copy(y_hbm.at[pl.ds(d_ref[k, t], 1), :],
                                     buf_ref.at[s, k, pl.ds(t, 1), :], sem_ref.at[s])

    def issue(d_ref, s):
        def body(t, carry):
            for k in range(TOP_K):
                row_copy(d_ref, s, t, k).start(priority=k % 2)
            return carry
        lax.fori_loop(0, tt, body, 0)

    @pl.when(i == 0)
    def _():
        issue(dcur_ref, 0)

    @pl.when(i + 1 < n)
    def _():
        issue(dnxt_ref, 1 - slot)

    def drain(t, carry):
        for k in range(TOP_K):
            row_copy(dcur_ref, slot, t, k).wait()
        return carry
    lax.fori_loop(0, tt, drain, 0)

    acc = jnp.zeros(x1_ref.shape, F32)
    for k in range(TOP_K):
        acc = acc + w_ref[:, k:k + 1] * buf_ref[slot, k]
    o_ref[...] = x1_ref[...] + mod_ref[0, 5:6, :] * _rms(acc, g_ref[...])


def _combine(dest, top_w_t, x1, mod3, g_post, y_rows, rows_per_batch, tt=128):
    N, D = x1.shape
    nt = N // tt
    tpb = rows_per_batch // tt
    return pl.pallas_call(
        _combine_body,
        grid=(nt,),
        in_specs=[pl.BlockSpec((TOP_K, tt), lambda i: (0, i), memory_space=pltpu.SMEM),
                  pl.BlockSpec((TOP_K, tt), lambda i: (0, jnp.minimum(i + 1, nt - 1)), memory_space=pltpu.SMEM),
                  pl.BlockSpec((tt, TOP_K), lambda i: (i, 0)),
                  pl.BlockSpec((tt, D), lambda i: (i, 0)),
                  pl.BlockSpec((1, 6, D), lambda i: (i // tpb, 0, 0)),
                  pl.BlockSpec((1, D), lambda i: (0, 0)),
                  pl.BlockSpec(memory_space=pl.ANY)],
        out_specs=pl.BlockSpec((tt, D), lambda i: (i, 0)),
        out_shape=jax.ShapeDtypeStruct((N, D), F32),
        scratch_shapes=[pltpu.VMEM((2, TOP_K, tt, D), F32),
                        pltpu.SemaphoreType.DMA((2,))],
        compiler_params=_cparams("arbitrary"),
        name="combine",
    )(dest, dest, top_w_t, x1, mod3, g_post, y_rows)


def _block(x, mod3, positions, g_pre_mix, g_post_mix, g_pre_ffn, g_post_ffn, w_in, b_in, conv_qk, sinks, g_mlstm,
           w_attn_proj, w_mlstm_proj, w_out, w_router, b_router, w_gate_up, b_gate_up, w_down, b_down, ffn_tm=512):
    B, S, D = x.shape
    N = B * S
    aq_w = w_attn_proj.shape[0]
    akv_w = (aq_w // ATTN_HEAD_DIM // ATTN_GROUP) * ATTN_HEAD_DIM
    mv_w = w_mlstm_proj.shape[0]
    mqk_w = mv_w // 2
    H = MLSTM_HEADS
    sizes = (aq_w, akv_w, akv_w, mqk_w, mqk_w, mv_w, mv_w, H, H, D, D)
    offs = [0]
    for sz in sizes:
        offs.append(offs[-1] + sz)
    seg_w = lambda k: w_in[:, offs[k]:offs[k + 1]]
    seg_b = lambda k: b_in[offs[k]:offs[k + 1]]
    order = (0, 3, 4, 5, 6, 9, 10)
    w_main = jnp.concatenate([seg_w(k) for k in order], axis=1).astype(BF16)
    b_main = jnp.concatenate([seg_b(k) for k in order]).reshape(1, -1)
    pad = LANES - 2 * H
    w_side = jnp.concatenate([seg_w(1), seg_w(2), seg_w(7), seg_w(8), jnp.zeros((D, pad), F32)], axis=1).astype(BF16)
    b_side = jnp.concatenate([seg_b(1), seg_b(2), seg_b(7), seg_b(8), jnp.zeros((pad,), F32)]).reshape(1, -1)

    x2 = x.reshape(N, D)
    proj, side = _in_proj(x2, mod3, g_pre_mix.reshape(1, D), w_main, b_main, w_side, b_side, S)

    c_mq = aq_w
    c_mv = aq_w + 2 * mqk_w
    c_mo = c_mv + mv_w
    c_ga = c_mo + mv_w
    c_gm = c_ga + D
    side_k, side_v, side_gates = 0, akv_w // LANES, 2 * akv_w // LANES

    half = ROPE_DIM // 2
    inv_freq = ROPE_THETA ** (-jnp.arange(half, dtype=F32) / half)
    lane_d = jnp.arange(LANES) % ATTN_HEAD_DIM
    inv_lane = jnp.where(lane_d < ROPE_DIM, inv_freq[lane_d % half], 0.0).reshape(1, LANES).astype(F32)
    pos_f = positions.astype(F32).reshape(N, 1)
    attn_o, mls_o = _mixers(proj, side, pos_f, inv_lane, sinks, conv_qk, g_mlstm.reshape(1, mv_w), B, S,
                            aq_w, side_k, side_v, c_mq // (2 * mqk_w), c_mv // mv_w, c_mo // mv_w, side_gates,
                            2 * mqk_w, mv_w)

    E = N_EXPERTS
    wr = jnp.concatenate([w_router, jnp.zeros((D, LANES - E), F32)], axis=1)
    br = jnp.concatenate([b_router, jnp.full((LANES - E,), -1e30, F32)]).reshape(1, LANES)
    merged = _merge(attn_o, mls_o, proj, w_attn_proj.astype(BF16), w_mlstm_proj.astype(BF16), c_ga // D, c_gm // D)
    x1, h2, logits = _out_proj(merged, x2, mod3, g_post_mix.reshape(1, D), g_pre_ffn.reshape(1, D),
                               w_out.astype(BF16), wr, br, S)

    top_e, top_w, rank, counts = _route(logits)
    cnt = counts[:, 0].astype(I32)
    padded = (cnt + ffn_tm - 1) // ffn_tm * ffn_tm
    pad_ends = jnp.cumsum(padded)
    pad_starts = pad_ends - padded
    n_rows = N * TOP_K + E * ffn_tm
    n_tiles = n_rows // ffn_tm
    tile_start = jnp.arange(n_tiles, dtype=I32) * ffn_tm
    tile_e = jnp.minimum(jnp.sum((pad_ends[None, :] <= tile_start[:, None]).astype(I32), axis=1), E - 1)
    n_valid = (pad_ends[E - 1:E] // ffn_tm).astype(I32)
    dest = _dest_rows(top_e, rank, jnp.broadcast_to(pad_starts.astype(F32)[:, None], (E, LANES)))

    tile_end = tile_start + ffn_tm
    zero_tiles = (jnp.any(pad_ends[None, :] == tile_end[:, None], axis=1)
                  | (tile_start >= pad_ends[E - 1])).astype(I32)
    xs = _dispatch(zero_tiles, dest, h2, n_rows, ffn_tm)
    act = _ffn_up(tile_e, n_valid, xs, w_gate_up, b_gate_up, ffn_tm)
    y_rows = _ffn_down(tile_e, n_valid, act, w_down, b_down, ffn_tm)
    out = _combine(dest, top_w.T, x1, mod3, g_post_ffn.reshape(1, D), y_rows, S)
    return out.reshape(B, S, D)


def kernel(x, c, positions, w_ada, b_ada, g_pre_mix, g_post_mix, g_pre_ffn, g_post_ffn, w_in, b_in, conv_qk, sinks,
           g_mlstm, w_attn_proj, w_mlstm_proj, w_out, w_router, b_router, w_gate_up, b_gate_up, w_down, b_down):
    B, S, D = x.shape
    depth = w_ada.shape[0]
    for l in range(depth):
        mod3 = _ada_ln(c, w_ada[l], b_ada[l]).reshape(B, 6, D)
        x = _block(x, mod3, positions, g_pre_mix[l], g_post_mix[l], g_pre_ffn[l], g_post_ffn[l], w_in[l], b_in[l],
                   conv_qk[l], sinks[l], g_mlstm[l], w_attn_proj[l], w_mlstm_proj[l], w_out[l], w_router[l],
                   b_router[l], w_gate_up[l], b_gate_up[l], w_down[l], b_down[l])
    return x
```
